```python
import jax, jax.numpy as jnp
from jax import lax
import numpy as np

D_MODEL = 1024
BATCH = 8
SEQ = 2048
DEPTH = 1
DEC_BATCH = 128
DEC_SEQ = 8
PAST_LEN = 16384
PAGE_SIZE = 128

D_MIX = D_MODEL
D_CONV = D_MIX // 2
D_POOL = D_MIX - D_CONV
CONV_HEADS = 8
CONV_WIDTH = 31
POOL_WINDOWS = (2, 4, 8, 16)
POOL_GROUPS = len(POOL_WINDOWS)
POOL_GROUP_DIM = D_POOL // POOL_GROUPS
POOL_MAX = max(POOL_WINDOWS)
D_FF = 2816
N_MOD = 9
HALF = 0.5
EPS = 1e-6

kernel_name = "hybrid_conformer_conv_multiscale_pool_decoder_step"


def _rmsnorm(x, g):
    xf = x.astype(jnp.float32)
    y = xf * lax.rsqrt(jnp.mean(xf * xf, axis=-1, keepdims=True) + EPS)
    return (y * g.astype(jnp.float32)).astype(x.dtype)


def _layernorm(x, g, b):
    xf = x.astype(jnp.float32)
    mu = jnp.mean(xf, axis=-1, keepdims=True)
    var = jnp.mean(jnp.square(xf - mu), axis=-1, keepdims=True)
    y = (xf - mu) * lax.rsqrt(var + EPS)
    return (y * g.astype(jnp.float32) + b.astype(jnp.float32)).astype(x.dtype)


def _swiglu(x, w_up, w_down):
    gu = x @ w_up
    g, u = jnp.split(gu, 2, axis=-1)
    return (jax.nn.silu(g) * u) @ w_down


def _depthwise_causal_conv(u_ext, w_dw, b_dw):
    c = u_ext.shape[-1]
    out = lax.conv_general_dilated(
        u_ext, w_dw[:, None, :], window_strides=(1,), padding='VALID',
        dimension_numbers=('NWC', 'WIO', 'NWC'), feature_group_count=c)
    return out + b_dw


def _pool_mix(u_ext, pos0, w_pool, pool_scale):
    bsz, text, _ = u_ext.shape
    t = text - (POOL_MAX - 1)
    uf = u_ext.astype(jnp.float32)
    s0 = jnp.concatenate([jnp.zeros((bsz, 1, D_POOL), jnp.float32), jnp.cumsum(uf, axis=1)], axis=1)
    pos = pos0 + jnp.arange(t, dtype=jnp.int32)
    u_cur = uf[:, POOL_MAX - 1:]
    outs = []
    for gi, w in enumerate(POOL_WINDOWS):
        sl = slice(gi * POOL_GROUP_DIM, (gi + 1) * POOL_GROUP_DIM)
        win_sum = s0[:, POOL_MAX:POOL_MAX + t, sl] - s0[:, POOL_MAX - w:POOL_MAX - w + t, sl]
        count = jnp.minimum(pos + 1, w).astype(jnp.float32)[None, :, None]
        outs.append(win_sum / count - u_cur[:, :, sl])
    pooled = jnp.stack(outs, axis=2).astype(u_ext.dtype)
    mixed = jnp.einsum('btgc,gcd->btgd', pooled, w_pool)
    return mixed.reshape(bsz, t, D_POOL) * pool_scale


def _layer(x, c, conv_prev, pool_prev, pos0,
           w_ada, b_ada, g_ffn1, w_ffn1_up, w_ffn1_down, g_mix, w_in, w_dw, b_dw,
           ln_g, ln_b, w_pool, pool_scale, w_out, g_ffn2, w_ffn2_up, w_ffn2_down):
    bsz = x.shape[0]
    mod = (jax.nn.silu(c) @ w_ada + b_ada).reshape(bsz, N_MOD, 1, D_MODEL)
    sh1, sc1, gt1 = mod[:, 0], mod[:, 1], mod[:, 2]
    shm, scm, gtm = mod[:, 3], mod[:, 4], mod[:, 5]
    sh2, sc2, gt2 = mod[:, 6], mod[:, 7], mod[:, 8]

    x = x + HALF * gt1 * _swiglu(_rmsnorm(x, g_ffn1) * (1 + sc1) + sh1, w_ffn1_up, w_ffn1_down)

    hm = _rmsnorm(x, g_mix) * (1 + scm) + shm
    proj = hm @ w_in
    a = proj[..., :D_CONV]
    g = proj[..., D_CONV:2 * D_CONV]
    u_pool = proj[..., 2 * D_CONV:]
    u_conv = a * jax.nn.sigmoid(g)
    conv_ext = jnp.concatenate([conv_prev, u_conv], axis=1)
    v = _depthwise_causal_conv(conv_ext, w_dw, b_dw)
    v = jax.nn.silu(_layernorm(v, ln_g, ln_b))
    pool_ext = jnp.concatenate([pool_prev, u_pool], axis=1)
    p = _pool_mix(pool_ext, pos0, w_pool, pool_scale)
    mix = jnp.concatenate([v, p], axis=-1) @ w_out
    x = x + gtm * mix

    x = x + HALF * gt2 * _swiglu(_rmsnorm(x, g_ffn2) * (1 + sc2) + sh2, w_ffn2_up, w_ffn2_down)
    return x, conv_ext[:, -(CONV_WIDTH - 1):], pool_ext[:, -(POOL_MAX - 1):]


def setup_inputs(seed: int = 0) -> dict:
    key = jax.random.key(seed)
    ks = jax.random.split(key, 26)

    def nrm(k, shape, scale=1.0):
        return jax.random.normal(k, shape, jnp.float32) * scale

    def gain(k, shape):
        return 1.0 + nrm(k, shape, 0.05)

    return {
        "x_prompt": nrm(ks[0], (BATCH, SEQ, D_MODEL)),
        "x_sample": nrm(ks[1], (DEC_BATCH, DEC_SEQ, D_MODEL)),
        "state_conv": nrm(ks[2], (DEPTH, DEC_BATCH, CONV_WIDTH - 1, D_CONV), 0.5),
        "state_pool": nrm(ks[3], (DEPTH, DEC_BATCH, POOL_MAX - 1, D_POOL)),
        "c_prompt": nrm(ks[4], (BATCH, D_MODEL)),
        "c_sample": nrm(ks[5], (DEC_BATCH, D_MODEL)),
        "w_ada": nrm(ks[6], (DEPTH, D_MODEL, N_MOD * D_MODEL), 0.02),
        "b_ada": nrm(ks[7], (DEPTH, N_MOD * D_MODEL), 0.01),
        "g_ffn1": gain(ks[8], (DEPTH, D_MODEL)),
        "w_ffn1_up": nrm(ks[9], (DEPTH, D_MODEL, 2 * D_FF), D_MODEL ** -0.5),
        "w_ffn1_down": nrm(ks[10], (DEPTH, D_FF, D_MODEL), D_FF ** -0.5),
        "g_mix": gain(ks[11], (DEPTH, D_MODEL)),
        "w_in": nrm(ks[12], (DEPTH, D_MODEL, 2 * D_CONV + D_POOL), D_MODEL ** -0.5),
        "w_dw": nrm(ks[13], (DEPTH, CONV_WIDTH, D_CONV), CONV_WIDTH ** -0.5),
        "b_dw": nrm(ks[14], (DEPTH, D_CONV), 0.01),
        "ln_g": gain(ks[15], (DEPTH, D_CONV)),
        "ln_b": nrm(ks[16], (DEPTH, D_CONV), 0.01),
        "w_pool": nrm(ks[17], (DEPTH, POOL_GROUPS, POOL_GROUP_DIM, POOL_GROUP_DIM), POOL_GROUP_DIM ** -0.5),
        "pool_scale": gain(ks[18], (DEPTH, D_POOL)),
        "w_out": nrm(ks[19], (DEPTH, D_MIX, D_MODEL), D_MIX ** -0.5),
        "g_ffn2": gain(ks[20], (DEPTH, D_MODEL)),
        "w_ffn2_up": nrm(ks[21], (DEPTH, D_MODEL, 2 * D_FF), D_MODEL ** -0.5),
        "w_ffn2_down": nrm(ks[22], (DEPTH, D_FF, D_MODEL), D_FF ** -0.5),
        "g_final": gain(ks[23], (D_MODEL,)),
    }


def reference(x_prompt, x_sample, state_conv, state_pool, c_prompt, c_sample,
              w_ada, b_ada, g_ffn1, w_ffn1_up, w_ffn1_down, g_mix, w_in, w_dw, b_dw,
              ln_g, ln_b, w_pool, pool_scale, w_out, g_ffn2, w_ffn2_up, w_ffn2_down, g_final):
    h_p, h_s = x_prompt, x_sample
    conv_p, pool_p, conv_s, pool_s = [], [], [], []
    for l in range(DEPTH):
        params = (w_ada[l], b_ada[l], g_ffn1[l], w_ffn1_up[l], w_ffn1_down[l], g_mix[l], w_in[l],
                  w_dw[l], b_dw[l], ln_g[l], ln_b[l], w_pool[l], pool_scale[l], w_out[l],
                  g_ffn2[l], w_ffn2_up[l], w_ffn2_down[l])
        bp = h_p.shape[0]
        zero_conv = jnp.zeros((bp, CONV_WIDTH - 1, D_CONV), h_p.dtype)
        zero_pool = jnp.zeros((bp, POOL_MAX - 1, D_POOL), h_p.dtype)
        h_p, cp, pp = _layer(h_p, c_prompt, zero_conv, zero_pool, 0, *params)
        h_s, cs, ps = _layer(h_s, c_sample, state_conv[l], state_pool[l], PAST_LEN, *params)
        conv_p.append(cp)
        pool_p.append(pp)
        conv_s.append(cs)
        pool_s.append(ps)
    y_prompt = _rmsnorm(h_p, g_final)
    y_sample = _rmsnorm(h_s, g_final)
    return (y_prompt, y_sample, jnp.stack(conv_p), jnp.stack(pool_p), jnp.stack(conv_s), jnp.stack(pool_s))
```

```python
import functools

import jax
import jax.numpy as jnp
from jax import lax
from jax.experimental import pallas as pl
from jax.experimental.pallas import tpu as pltpu

EPS = 1e-6
HALF = 0.5
N_MOD = 9
CONV_WIDTH = 31
POOL_WINDOWS = (2, 4, 8, 16)
POOL_MAX = max(POOL_WINDOWS)
SAMPLE_POS0 = 16384

SUBLANES = 8
CONV_HALO = 32
POOL_HALO = 16
VMEM_LIMIT_BYTES = 56 * 1024 * 1024

F32 = jnp.float32
BF16 = jnp.bfloat16


def _dot(a, b):
    return jnp.dot(a, b, preferred_element_type=F32)


def _sigmoid(x):
    return 1.0 / (1.0 + jnp.exp(-x))


def _rms_scale(x):
    return lax.rsqrt(jnp.mean(x * x, axis=-1, keepdims=True) + EPS)


def _const_spec(shape):
    zeros = (0,) * len(shape)
    return pl.BlockSpec(shape, lambda *_: zeros, pipeline_mode=pl.Buffered(1))


def _adaln_kernel(cp_ref, cs_ref, w_ref, b_ref, op_ref, os_ref):
    w = w_ref[...].astype(BF16)
    b = b_ref[0]

    def mod(c):
        s = c * _sigmoid(c)
        return _dot(s.astype(BF16), w) + b

    op_ref[0] = mod(cp_ref[...])
    os_ref[0] = mod(cs_ref[...])


def _adaln(c_prompt, c_sample, w_ada, b_ada):
    bp, d = c_prompt.shape
    bs = c_sample.shape[0]
    return pl.pallas_call(
        _adaln_kernel,
        grid=(N_MOD,),
        in_specs=[
            pl.BlockSpec((bp, d), lambda j: (0, 0)),
            pl.BlockSpec((bs, d), lambda j: (0, 0)),
            pl.BlockSpec((d, d), lambda j: (0, j)),
            pl.BlockSpec((1, 1, d), lambda j: (j, 0, 0)),
        ],
        out_specs=[
            pl.BlockSpec((1, bp, d), lambda j: (j, 0, 0)),
            pl.BlockSpec((1, bs, d), lambda j: (j, 0, 0)),
        ],
        out_shape=[
            jax.ShapeDtypeStruct((N_MOD, bp, d), F32),
            jax.ShapeDtypeStruct((N_MOD, bs, d), F32),
        ],
        compiler_params=pltpu.CompilerParams(
            dimension_semantics=("arbitrary",), vmem_limit_bytes=VMEM_LIMIT_BYTES),
        name="adaln",
    )(c_prompt, c_sample, w_ada, b_ada.reshape(N_MOD, 1, d))


def _ffn_kernel(x_ref, mod_ref, g_ref, wup_ref, wdn_ref, *rest, ff_chunk, final_norm):
    if final_norm:
        gf_ref, o_ref, xb_scr, a_scr = rest
    else:
        o_ref, xb_scr, a_scr = rest
    bb, tt, d = x_ref.shape
    m = bb * tt
    d_ff = wdn_ref.shape[0]

    x = x_ref[...]
    shift, scale, gate = mod_ref[0], mod_ref[1], mod_ref[2]
    xn = (x * _rms_scale(x)) * g_ref[...] * (1.0 + scale) + shift
    xb_scr[...] = xn.reshape(m, d).astype(BF16)

    for lo in range(0, d_ff, ff_chunk):
        xb = xb_scr[...]
        g = _dot(xb, wup_ref[:, lo:lo + ff_chunk])
        u = _dot(xb, wup_ref[:, d_ff + lo:d_ff + lo + ff_chunk])
        a_scr[:, lo:lo + ff_chunk] = (g * _sigmoid(g) * u).astype(BF16)

    y = _dot(a_scr[...], wdn_ref[...]).reshape(bb, tt, d)
    out = x + (HALF * gate) * y
    if final_norm:
        out = (out * _rms_scale(out)) * gf_ref[...]
    o_ref[...] = out


def _ffn(x, mod4, mod_block, gain, w_up, w_down, g_final, *, batch_block, time_block, ff_chunk=256):
    b, t, d = x.shape
    d_ff = w_down.shape[0]
    m = batch_block * time_block
    final_norm = g_final is not None
    x_spec = pl.BlockSpec((batch_block, time_block, d), lambda i, j: (i, j, 0))
    in_specs = [
        x_spec,
        pl.BlockSpec((3, batch_block, 1, d), lambda i, j: (mod_block, i, 0, 0)),
        _const_spec((1, d)),
        _const_spec((d, 2 * d_ff)),
        _const_spec((d_ff, d)),
    ]
    args = [x, mod4, gain, w_up, w_down]
    if final_norm:
        in_specs.append(_const_spec((1, d)))
        args.append(g_final)
    return pl.pallas_call(
        functools.partial(_ffn_kernel, ff_chunk=ff_chunk, final_norm=final_norm),
        grid=(b // batch_block, t // time_block),
        in_specs=in_specs,
        out_specs=x_spec,
        out_shape=jax.ShapeDtypeStruct(x.shape, F32),
        scratch_shapes=[pltpu.VMEM((m, d), BF16), pltpu.VMEM((m, d_ff), BF16)],
        compiler_params=pltpu.CompilerParams(
            dimension_semantics=("arbitrary", "arbitrary"), vmem_limit_bytes=VMEM_LIMIT_BYTES),
        name="ffn_final" if final_norm else "ffn",
    )(*args)


def _layernorm_silu(v, ln_g, ln_b):
    mu = jnp.mean(v, axis=-1, keepdims=True)
    c = v - mu
    var = jnp.mean(c * c, axis=-1, keepdims=True)
    y = (c * lax.rsqrt(var + EPS)) * ln_g + ln_b
    return y * _sigmoid(y)


def _mixer_prompt_kernel(x_ref, mod_ref, g_ref, win_ref, wdw_ref, bdw_ref, lng_ref, lnb_ref,
                         wpool_ref, pscale_ref, wout_ref,
                         o_ref, oconv_ref, opool_ref,
                         cs, ps, v_scr, mix_scr, *, row_block):
    t_idx = pl.program_id(1)
    n_t = pl.num_programs(1)
    tm, d = x_ref.shape[1], x_ref.shape[2]
    d_conv = cs.shape[1]
    group = d_conv // len(POOL_WINDOWS)

    @pl.when(t_idx == 0)
    def _():
        cs[0:CONV_HALO, :] = jnp.zeros((CONV_HALO, d_conv), F32)
        ps[0:POOL_HALO, :] = jnp.zeros((POOL_HALO, d_conv), F32)

    x = x_ref[0]
    shift, scale, gate = mod_ref[0, 0], mod_ref[1, 0], mod_ref[2, 0]
    hm = (x * _rms_scale(x)) * g_ref[...] * (1.0 + scale) + shift
    proj = _dot(hm.astype(BF16), win_ref[...])
    cs[CONV_HALO:CONV_HALO + tm, :] = proj[:, :d_conv] * _sigmoid(proj[:, d_conv:2 * d_conv])
    ps[POOL_HALO:POOL_HALO + tm, :] = proj[:, 2 * d_conv:]

    first = CONV_HALO - (CONV_WIDTH - 1)
    for r0 in range(0, tm, row_block):
        acc = jnp.broadcast_to(bdw_ref[...], (row_block, d_conv))
        for k in range(CONV_WIDTH):
            acc = acc + wdw_ref[k:k + 1, :] * cs[first + k + r0:first + k + r0 + row_block, :]
        v_scr[r0:r0 + row_block, :] = acc
    mix_scr[:, 0:d_conv] = _layernorm_silu(v_scr[...], lng_ref[...], lnb_ref[...]).astype(BF16)

    pos = t_idx * tm + lax.broadcasted_iota(jnp.int32, (tm, 1), 0)
    for gi, w in enumerate(POOL_WINDOWS):
        lanes = slice(gi * group, (gi + 1) * group)
        win_sum = ps[POOL_HALO:POOL_HALO + tm, lanes]
        for j in range(1, w):
            win_sum = win_sum + ps[POOL_HALO - j:POOL_HALO - j + tm, lanes]
        count = jnp.minimum(pos + 1, w).astype(F32)
        pooled = win_sum / count - ps[POOL_HALO:POOL_HALO + tm, lanes]
        mixed = _dot(pooled.astype(BF16), wpool_ref[gi])
        mix_scr[:, d_conv + gi * group:d_conv + (gi + 1) * group] = (
            mixed * pscale_ref[:, lanes]).astype(BF16)

    o_ref[0] = x + gate * _dot(mix_scr[...], wout_ref[...])

    @pl.when(t_idx == n_t - 1)
    def _():
        oconv_ref[0, 0] = cs[CONV_HALO + tm - (CONV_WIDTH - 1):CONV_HALO + tm, :]
        opool_ref[0, 0] = ps[POOL_HALO + tm - (POOL_MAX - 1):POOL_HALO + tm, :]

    cs[0:CONV_HALO, :] = cs[tm:tm + CONV_HALO, :]
    ps[0:POOL_HALO, :] = ps[tm:tm + POOL_HALO, :]


def _mixer_prompt(x, mod4, gain, w_in, w_dw, b_dw, ln_g, ln_b, w_pool, pool_scale, w_out,
                  *, time_block, row_block=64):
    b, t, d = x.shape
    d_conv = w_dw.shape[1]
    d_pool = pool_scale.shape[1]
    n_groups, group = w_pool.shape[0], w_pool.shape[1]
    x_spec = pl.BlockSpec((1, time_block, d), lambda i, j: (i, j, 0))
    return pl.pallas_call(
        functools.partial(_mixer_prompt_kernel, row_block=row_block),
        grid=(b, t // time_block),
        in_specs=[
            x_spec,
            pl.BlockSpec((3, 1, 1, d), lambda i, j: (1, i, 0, 0)),
            _const_spec((1, d)),
            _const_spec((d, 2 * d_conv + d_pool)),
            _const_spec((CONV_WIDTH, d_conv)),
            _const_spec((1, d_conv)),
            _const_spec((1, d_conv)),
            _const_spec((1, d_conv)),
            _const_spec((n_groups, group, group)),
            _const_spec((1, d_pool)),
            _const_spec((d_conv + d_pool, d)),
        ],
        out_specs=[
            x_spec,
            pl.BlockSpec((1, 1, CONV_WIDTH - 1, d_conv), lambda i, j: (0, i, 0, 0)),
            pl.BlockSpec((1, 1, POOL_MAX - 1, d_pool), lambda i, j: (0, i, 0, 0)),
        ],
        out_shape=[
            jax.ShapeDtypeStruct(x.shape, F32),
            jax.ShapeDtypeStruct((1, b, CONV_WIDTH - 1, d_conv), F32),
            jax.ShapeDtypeStruct((1, b, POOL_MAX - 1, d_pool), F32),
        ],
        scratch_shapes=[
            pltpu.VMEM((CONV_HALO + time_block, d_conv), F32),
            pltpu.VMEM((POOL_HALO + time_block, d_pool), F32),
            pltpu.VMEM((time_block, d_conv), F32),
            pltpu.VMEM((time_block, d_conv + d_pool), BF16),
        ],
        compiler_params=pltpu.CompilerParams(
            dimension_semantics=("arbitrary", "arbitrary"), vmem_limit_bytes=VMEM_LIMIT_BYTES),
        name="mixer_prompt",
    )(x, mod4, gain, w_in, w_dw, b_dw, ln_g, ln_b, w_pool, pool_scale, w_out)


def _mixer_sample_kernel(x_ref, mod_ref, g_ref, win_ref, wdw_ref, bdw_ref, lng_ref, lnb_ref,
                         wpool_ref, pscale_ref, wout_ref, sconv_ref, spool_ref,
                         o_ref, oconv_ref, opool_ref,
                         cs, ps, mix_scr):
    bb, tt, d = x_ref.shape
    m = bb * tt
    d_conv = cs.shape[2]
    group = d_conv // len(POOL_WINDOWS)
    n_conv = CONV_WIDTH - 1
    n_pool = POOL_MAX - 1

    x = x_ref[...]
    shift, scale, gate = mod_ref[0], mod_ref[1], mod_ref[2]
    hm = (x * _rms_scale(x)) * g_ref[...] * (1.0 + scale) + shift
    proj = _dot(hm.reshape(m, d).astype(BF16), win_ref[...])
    u_conv = proj[:, :d_conv] * _sigmoid(proj[:, d_conv:2 * d_conv])
    cs[:, 0:n_conv, :] = sconv_ref[0]
    cs[:, n_conv:n_conv + tt, :] = u_conv.reshape(bb, tt, d_conv)
    ps[:, 0:n_pool, :] = spool_ref[0]
    ps[:, n_pool:n_pool + tt, :] = proj[:, 2 * d_conv:].reshape(bb, tt, d_conv)

    oconv_ref[0] = cs[:, tt:tt + n_conv, :]
    opool_ref[0] = ps[:, tt:tt + n_pool, :]

    acc = jnp.broadcast_to(bdw_ref[...].reshape(1, 1, d_conv), (bb, tt, d_conv))
    for k in range(CONV_WIDTH):
        acc = acc + wdw_ref[k:k + 1, :].reshape(1, 1, d_conv) * cs[:, k:k + tt, :]
    v = _layernorm_silu(acc.reshape(m, d_conv), lng_ref[...], lnb_ref[...])
    mix_scr[:, 0:d_conv] = v.astype(BF16)

    for gi, w in enumerate(POOL_WINDOWS):
        lanes = slice(gi * group, (gi + 1) * group)
        win_sum = ps[:, n_pool:n_pool + tt, lanes]
        for j in range(1, w):
            win_sum = win_sum + ps[:, n_pool - j:n_pool - j + tt, lanes]
        count = min(SAMPLE_POS0 + 1, w)
        pooled = win_sum / float(count) - ps[:, n_pool:n_pool + tt, lanes]
        mixed = _dot(pooled.reshape(m, group).astype(BF16), wpool_ref[gi])
        mix_scr[:, d_conv + gi * group:d_conv + (gi + 1) * group] = (
            mixed * pscale_ref[:, lanes]).astype(BF16)

    o_ref[...] = x + gate * _dot(mix_scr[...], wout_ref[...]).reshape(bb, tt, d)


def _mixer_sample(x, mod4, gain, w_in, w_dw, b_dw, ln_g, ln_b, w_pool, pool_scale, w_out,
                  state_conv, state_pool, *, batch_block):
    b, t, d = x.shape
    d_conv = w_dw.shape[1]
    d_pool = pool_scale.shape[1]
    n_groups, group = w_pool.shape[0], w_pool.shape[1]
    x_spec = pl.BlockSpec((batch_block, t, d), lambda i: (i, 0, 0))
    conv_spec = pl.BlockSpec((1, batch_block, CONV_WIDTH - 1, d_conv), lambda i: (0, i, 0, 0))
    pool_spec = pl.BlockSpec((1, batch_block, POOL_MAX - 1, d_pool), lambda i: (0, i, 0, 0))
    pad8 = lambda n: -(-n // SUBLANES) * SUBLANES
    return pl.pallas_call(
        _mixer_sample_kernel,
        grid=(b // batch_block,),
        in_specs=[
            x_spec,
            pl.BlockSpec((3, batch_block, 1, d), lambda i: (1, i, 0, 0)),
            _const_spec((1, d)),
            _const_spec((d, 2 * d_conv + d_pool)),
            _const_spec((CONV_WIDTH, d_conv)),
            _const_spec((1, d_conv)),
            _const_spec((1, d_conv)),
            _const_spec((1, d_conv)),
            _const_spec((n_groups, group, group)),
            _const_spec((1, d_pool)),
            _const_spec((d_conv + d_pool, d)),
            conv_spec,
            pool_spec,
        ],
        out_specs=[x_spec, conv_spec, pool_spec],
        out_shape=[
            jax.ShapeDtypeStruct(x.shape, F32),
            jax.ShapeDtypeStruct(state_conv.shape, F32),
            jax.ShapeDtypeStruct(state_pool.shape, F32),
        ],
        scratch_shapes=[
            pltpu.VMEM((batch_block, pad8(CONV_WIDTH - 1 + t), d_conv), F32),
            pltpu.VMEM((batch_block, pad8(POOL_MAX - 1 + t), d_pool), F32),
            pltpu.VMEM((batch_block * t, d_conv + d_pool), BF16),
        ],
        compiler_params=pltpu.CompilerParams(
            dimension_semantics=("arbitrary",), vmem_limit_bytes=VMEM_LIMIT_BYTES),
        name="mixer_sample",
    )(x, mod4, gain, w_in, w_dw, b_dw, ln_g, ln_b, w_pool, pool_scale, w_out, state_conv, state_pool)


def kernel(x_prompt, x_sample, state_conv, state_pool, c_prompt, c_sample, w_ada, b_ada, g_ffn1, w_ffn1_up, w_ffn1_down, g_mix, w_in, w_dw, b_dw, ln_g, ln_b, w_pool, pool_scale, w_out, g_ffn2, w_ffn2_up, w_ffn2_down, g_final):
    depth = w_ada.shape[0]
    assert depth == 1 and state_conv.shape[0] == 1 and state_pool.shape[0] == 1
    bp, tp, d = x_prompt.shape
    bs, ts, _ = x_sample.shape
    assert ts == SUBLANES, "a sample sequence must fill exactly one sublane tile"

    mod_p, mod_s = _adaln(c_prompt, c_sample, w_ada[0], b_ada[0])
    mod_p = mod_p.reshape(N_MOD, bp, 1, d)
    mod_s = mod_s.reshape(N_MOD, bs, 1, d)

    bf = lambda w: w[0].astype(BF16)
    w1u, w1d, w2u, w2d = bf(w_ffn1_up), bf(w_ffn1_down), bf(w_ffn2_up), bf(w_ffn2_down)
    mixer_w = (g_mix, bf(w_in), w_dw[0], b_dw, ln_g, ln_b, bf(w_pool), pool_scale, bf(w_out))
    g_final2 = g_final.reshape(1, d)

    prompt_tile = dict(batch_block=1, time_block=min(tp, 1024))
    sample_tile = dict(batch_block=min(bs, 64), time_block=ts)

    h_p = _ffn(x_prompt, mod_p, 0, g_ffn1, w1u, w1d, None, **prompt_tile)
    h_s = _ffn(x_sample, mod_s, 0, g_ffn1, w1u, w1d, None, **sample_tile)

    h_p, conv_p, pool_p = _mixer_prompt(h_p, mod_p, *mixer_w, time_block=min(tp, 512))
    h_s, conv_s, pool_s = _mixer_sample(h_s, mod_s, *mixer_w, state_conv, state_pool,
                                        batch_block=min(bs, 32))

    y_p = _ffn(h_p, mod_p, 2, g_ffn2, w2u, w2d, g_final2, **prompt_tile)
    y_s = _ffn(h_s, mod_s, 2, g_ffn2, w2u, w2d, g_final2, **sample_tile)
    return (y_p, y_s, conv_p, pool_p, conv_s, pool_s)
```

```python
import functools

import jax
import jax.numpy as jnp
from jax import lax
from jax.experimental import pallas as pl
from jax.experimental.pallas import tpu as pltpu

EPS = 1e-6
HALF = 0.5
N_MOD = 9
CONV_WIDTH = 31
POOL_WINDOWS = (2, 4, 8, 16)
POOL_MAX = max(POOL_WINDOWS)
SAMPLE_POS0 = 16384

LANES = 128
CONV_HALO = 32
POOL_HALO = 16
VMEM_LIMIT_BYTES = 56 * 1024 * 1024

F32 = jnp.float32
BF16 = jnp.bfloat16


def _dot(a, b):
    return jnp.dot(a, b, preferred_element_type=F32)


def _sigmoid(x):
    return 1.0 / (1.0 + jnp.exp(-x))


def _rms_scale(x):
    return lax.rsqrt(jnp.mean(x * x, axis=-1, keepdims=True) + EPS)


def _const_spec(shape):
    zeros = (0,) * len(shape)
    return pl.BlockSpec(shape, lambda *_: zeros, pipeline_mode=pl.Buffered(1))


def _adaln_kernel(cp_ref, cs_ref, w_ref, b_ref, op_ref, os_ref):
    w = w_ref[...].astype(BF16)
    b = b_ref[0]

    def mod(c):
        s = c * _sigmoid(c)
        return _dot(s.astype(BF16), w) + b

    op_ref[0] = mod(cp_ref[...])
    os_ref[0] = mod(cs_ref[...])


def _adaln(c_prompt, c_sample, w_ada, b_ada):
    bp, d = c_prompt.shape
    bs = c_sample.shape[0]
    return pl.pallas_call(
        _adaln_kernel,
        grid=(N_MOD,),
        in_specs=[
            pl.BlockSpec((bp, d), lambda j: (0, 0)),
            pl.BlockSpec((bs, d), lambda j: (0, 0)),
            pl.BlockSpec((d, d), lambda j: (0, j)),
            pl.BlockSpec((1, 1, d), lambda j: (j, 0, 0)),
        ],
        out_specs=[
            pl.BlockSpec((1, bp, d), lambda j: (j, 0, 0)),
            pl.BlockSpec((1, bs, d), lambda j: (j, 0, 0)),
        ],
        out_shape=[
            jax.ShapeDtypeStruct((N_MOD, bp, d), F32),
            jax.ShapeDtypeStruct((N_MOD, bs, d), F32),
        ],
        compiler_params=pltpu.CompilerParams(
            dimension_semantics=("arbitrary",), vmem_limit_bytes=VMEM_LIMIT_BYTES),
        name="adaln",
    )(c_prompt, c_sample, w_ada, b_ada.reshape(N_MOD, 1, d))


def _ffn_kernel(x_ref, mod_ref, g_ref, wup_ref, wdn_ref, *rest, ff_chunk, final_norm):
    if final_norm:
        gf_ref, o_ref, xb_scr, a_scr = rest
    else:
        o_ref, xb_scr, a_scr = rest
    lead, rows, d = x_ref.shape
    m = lead * rows
    d_ff = wdn_ref.shape[0]

    x = x_ref[...]
    shift, scale, gate = mod_ref[0], mod_ref[1], mod_ref[2]
    xn = (x * _rms_scale(x)) * g_ref[...] * (1.0 + scale) + shift
    xb_scr[...] = xn.reshape(m, d).astype(BF16)

    for lo in range(0, d_ff, ff_chunk):
        xb = xb_scr[...]
        g = _dot(xb, wup_ref[:, lo:lo + ff_chunk])
        u = _dot(xb, wup_ref[:, d_ff + lo:d_ff + lo + ff_chunk])
        a_scr[:, lo:lo + ff_chunk] = (g * _sigmoid(g) * u).astype(BF16)

    y = _dot(a_scr[...], wdn_ref[...]).reshape(lead, rows, d)
    out = x + (HALF * gate) * y
    if final_norm:
        out = (out * _rms_scale(out)) * gf_ref[...]
    o_ref[...] = out


def _ffn(x, x_spec, mod4, mod_spec, grid, gain, w_up, w_down, g_final, *, ff_chunk=256):
    d = x.shape[-1]
    d_ff = w_down.shape[0]
    m = x_spec.block_shape[0] * x_spec.block_shape[1]
    final_norm = g_final is not None
    in_specs = [x_spec, mod_spec, _const_spec((1, d)), _const_spec((d, 2 * d_ff)), _const_spec((d_ff, d))]
    args = [x, mod4, gain, w_up, w_down]
    if final_norm:
        in_specs.append(_const_spec((1, d)))
        args.append(g_final)
    return pl.pallas_call(
        functools.partial(_ffn_kernel, ff_chunk=ff_chunk, final_norm=final_norm),
        grid=grid,
        in_specs=in_specs,
        out_specs=x_spec,
        out_shape=jax.ShapeDtypeStruct(x.shape, F32),
        scratch_shapes=[pltpu.VMEM((m, d), BF16), pltpu.VMEM((m, d_ff), BF16)],
        compiler_params=pltpu.CompilerParams(
            dimension_semantics=("arbitrary",) * len(grid), vmem_limit_bytes=VMEM_LIMIT_BYTES),
        name="ffn_final" if final_norm else "ffn",
    )(*args)


def _layernorm_silu(v, ln_g, ln_b):
    mu = jnp.mean(v, axis=-1, keepdims=True)
    c = v - mu
    var = jnp.mean(c * c, axis=-1, keepdims=True)
    y = (c * lax.rsqrt(var + EPS)) * ln_g + ln_b
    return y * _sigmoid(y)


def _mixer_prompt_kernel(x_ref, mod_ref, g_ref, win_ref, wdw_ref, bdw_ref, lng_ref, lnb_ref,
                         wpool_ref, pscale_ref, wout_ref,
                         o_ref, oconv_ref, opool_ref,
                         cs, ps, v_scr, mix_scr, *, row_block):
    t_idx = pl.program_id(1)
    n_t = pl.num_programs(1)
    tm, d = x_ref.shape[1], x_ref.shape[2]
    n_slabs = cs.shape[0]
    d_conv = n_slabs * LANES
    assert len(POOL_WINDOWS) == ps.shape[0] and wpool_ref.shape[1] == LANES

    @pl.when(t_idx == 0)
    def _():
        cs[:, 0:CONV_HALO, :] = jnp.zeros((n_slabs, CONV_HALO, LANES), F32)
        ps[:, 0:POOL_HALO, :] = jnp.zeros((ps.shape[0], POOL_HALO, LANES), F32)

    x = x_ref[0]
    shift, scale, gate = mod_ref[0, 0], mod_ref[1, 0], mod_ref[2, 0]
    hm = (x * _rms_scale(x)) * g_ref[...] * (1.0 + scale) + shift
    proj = _dot(hm.astype(BF16), win_ref[...])
    u_conv = proj[:, :d_conv] * _sigmoid(proj[:, d_conv:2 * d_conv])
    for s in range(n_slabs):
        cs[s, CONV_HALO:CONV_HALO + tm, :] = u_conv[:, s * LANES:(s + 1) * LANES]
        ps[s, POOL_HALO:POOL_HALO + tm, :] = proj[:, 2 * d_conv + s * LANES:2 * d_conv + (s + 1) * LANES]

    first = CONV_HALO - (CONV_WIDTH - 1)
    for s in range(n_slabs):
        lanes = slice(s * LANES, (s + 1) * LANES)
        for r0 in range(0, tm, row_block):
            acc = jnp.broadcast_to(bdw_ref[:, lanes], (row_block, LANES))
            for k in range(CONV_WIDTH):
                acc = acc + wdw_ref[k:k + 1, lanes] * cs[s, first + k + r0:first + k + r0 + row_block, :]
            v_scr[r0:r0 + row_block, lanes] = acc
    mix_scr[:, 0:d_conv] = _layernorm_silu(v_scr[...], lng_ref[...], lnb_ref[...]).astype(BF16)

    pos = t_idx * tm + lax.broadcasted_iota(jnp.int32, (tm, 1), 0)
    for gi, w in enumerate(POOL_WINDOWS):
        lanes = slice(gi * LANES, (gi + 1) * LANES)
        cur = ps[gi, POOL_HALO:POOL_HALO + tm, :]
        win_sum = cur
        for j in range(1, w):
            win_sum = win_sum + ps[gi, POOL_HALO - j:POOL_HALO - j + tm, :]
        count = jnp.minimum(pos + 1, w).astype(F32)
        pooled = win_sum / count - cur
        mixed = _dot(pooled.astype(BF16), wpool_ref[gi])
        mix_scr[:, d_conv + gi * LANES:d_conv + (gi + 1) * LANES] = (
            mixed * pscale_ref[:, lanes]).astype(BF16)

    o_ref[0] = x + gate * _dot(mix_scr[...], wout_ref[...])

    @pl.when(t_idx == n_t - 1)
    def _():
        for s in range(n_slabs):
            lanes = slice(s * LANES, (s + 1) * LANES)
            oconv_ref[0, 0, :, lanes] = cs[s, CONV_HALO + tm - (CONV_WIDTH - 1):CONV_HALO + tm, :]
            opool_ref[0, 0, :, lanes] = ps[s, POOL_HALO + tm - (POOL_MAX - 1):POOL_HALO + tm, :]

    cs[:, 0:CONV_HALO, :] = cs[:, tm:tm + CONV_HALO, :]
    ps[:, 0:POOL_HALO, :] = ps[:, tm:tm + POOL_HALO, :]


def _mixer_prompt(x, mod4, gain, w_in, w_dw, b_dw, ln_g, ln_b, w_pool, pool_scale, w_out,
                  *, time_block, row_block=64):
    b, t, d = x.shape
    d_conv = w_dw.shape[1]
    d_pool = pool_scale.shape[1]
    n_groups, group = w_pool.shape[0], w_pool.shape[1]
    assert group == LANES and d_conv % LANES == 0 and d_pool == n_groups * LANES
    x_spec = pl.BlockSpec((1, time_block, d), lambda i, j: (i, j, 0))
    return pl.pallas_call(
        functools.partial(_mixer_prompt_kernel, row_block=row_block),
        grid=(b, t // time_block),
        in_specs=[
            x_spec,
            pl.BlockSpec((3, 1, 1, d), lambda i, j: (1, i, 0, 0)),
            _const_spec((1, d)),
            _const_spec((d, 2 * d_conv + d_pool)),
            _const_spec((CONV_WIDTH, d_conv)),
            _const_spec((1, d_conv)),
            _const_spec((1, d_conv)),
            _const_spec((1, d_conv)),
            _const_spec((n_groups, group, group)),
            _const_spec((1, d_pool)),
            _const_spec((d_conv + d_pool, d)),
        ],
        out_specs=[
            x_spec,
            pl.BlockSpec((1, 1, CONV_WIDTH - 1, d_conv), lambda i, j: (0, i, 0, 0)),
            pl.BlockSpec((1, 1, POOL_MAX - 1, d_pool), lambda i, j: (0, i, 0, 0)),
        ],
        out_shape=[
            jax.ShapeDtypeStruct(x.shape, F32),
            jax.ShapeDtypeStruct((1, b, CONV_WIDTH - 1, d_conv), F32),
            jax.ShapeDtypeStruct((1, b, POOL_MAX - 1, d_pool), F32),
        ],
        scratch_shapes=[
            pltpu.VMEM((d_conv // LANES, CONV_HALO + time_block, LANES), F32),
            pltpu.VMEM((n_groups, POOL_HALO + time_block, LANES), F32),
            pltpu.VMEM((time_block, d_conv), F32),
            pltpu.VMEM((time_block, d_conv + d_pool), BF16),
        ],
        compiler_params=pltpu.CompilerParams(
            dimension_semantics=("arbitrary", "arbitrary"), vmem_limit_bytes=VMEM_LIMIT_BYTES),
        name="mixer_prompt",
    )(x, mod4, gain, w_in, w_dw, b_dw, ln_g, ln_b, w_pool, pool_scale, w_out)


def _mixer_sample_kernel(x_ref, mod_ref, g_ref, win_ref, wdw_ref, bdw_ref, lng_ref, lnb_ref,
                         wpool_ref, pscale_ref, wout_ref, sconv_ref, spool_ref,
                         o_ref, oconv_ref, opool_ref, mix_scr):
    tt, bb, d = x_ref.shape
    m = tt * bb
    n_conv, d_conv = sconv_ref.shape[1], sconv_ref.shape[3]
    n_pool = spool_ref.shape[1]

    x = x_ref[...]
    shift, scale, gate = mod_ref[0], mod_ref[1], mod_ref[2]
    hm = (x * _rms_scale(x)) * g_ref[...] * (1.0 + scale) + shift
    proj = _dot(hm.reshape(m, d).astype(BF16), win_ref[...])
    u_conv = proj[:, :d_conv] * _sigmoid(proj[:, d_conv:2 * d_conv])
    u_pool = proj[:, 2 * d_conv:]

    def conv_in(s):
        return sconv_ref[0, s] if s < n_conv else u_conv[(s - n_conv) * bb:(s - n_conv + 1) * bb, :]

    def pool_in(s, lanes=slice(None)):
        return spool_ref[0, s, :, lanes] if s < n_pool else u_pool[(s - n_pool) * bb:(s - n_pool + 1) * bb, lanes]

    for s in range(n_conv):
        oconv_ref[0, s] = conv_in(s + tt)
    for s in range(n_pool):
        opool_ref[0, s] = pool_in(s + tt)

    for t in range(tt):
        acc = jnp.broadcast_to(bdw_ref[...], (bb, d_conv))
        for k in range(CONV_WIDTH):
            acc = acc + wdw_ref[k:k + 1, :] * conv_in(t + k)
        mix_scr[t * bb:(t + 1) * bb, 0:d_conv] = _layernorm_silu(acc, lng_ref[...], lnb_ref[...]).astype(BF16)

    for gi, w in enumerate(POOL_WINDOWS):
        lanes = slice(gi * LANES, (gi + 1) * LANES)
        count = float(min(SAMPLE_POS0 + 1, w))
        for t in range(tt):
            cur = pool_in(n_pool + t, lanes)
            win_sum = cur
            for j in range(1, w):
                win_sum = win_sum + pool_in(n_pool + t - j, lanes)
            pooled = win_sum / count - cur
            mixed = _dot(pooled.astype(BF16), wpool_ref[gi])
            mix_scr[t * bb:(t + 1) * bb, d_conv + gi * LANES:d_conv + (gi + 1) * LANES] = (
                mixed * pscale_ref[:, lanes]).astype(BF16)

    o_ref[...] = x + gate * _dot(mix_scr[...], wout_ref[...]).reshape(tt, bb, d)


def _mixer_sample(x, mod4, gain, w_in, w_dw, b_dw, ln_g, ln_b, w_pool, pool_scale, w_out,
                  state_conv, state_pool, *, batch_block):
    t, b, d = x.shape
    d_conv = w_dw.shape[1]
    d_pool = pool_scale.shape[1]
    n_groups, group = w_pool.shape[0], w_pool.shape[1]
    assert group == LANES and d_pool == n_groups * LANES and d_conv == d_pool
    x_spec = pl.BlockSpec((t, batch_block, d), lambda i: (0, i, 0))
    conv_spec = pl.BlockSpec((1, CONV_WIDTH - 1, batch_block, d_conv), lambda i: (0, 0, i, 0))
    pool_spec = pl.BlockSpec((1, POOL_MAX - 1, batch_block, d_pool), lambda i: (0, 0, i, 0))
    return pl.pallas_call(
        _mixer_sample_kernel,
        grid=(b // batch_block,),
        in_specs=[
            x_spec,
            pl.BlockSpec((3, 1, batch_block, d), lambda i: (1, 0, i, 0)),
            _const_spec((1, d)),
            _const_spec((d, 2 * d_conv + d_pool)),
            _const_spec((CONV_WIDTH, d_conv)),
            _const_spec((1, d_conv)),
            _const_spec((1, d_conv)),
            _const_spec((1, d_conv)),
            _const_spec((n_groups, group, group)),
            _const_spec((1, d_pool)),
            _const_spec((d_conv + d_pool, d)),
            conv_spec,
            pool_spec,
        ],
        out_specs=[x_spec, conv_spec, pool_spec],
        out_shape=[
            jax.ShapeDtypeStruct(x.shape, F32),
            jax.ShapeDtypeStruct(state_conv.shape, F32),
            jax.ShapeDtypeStruct(state_pool.shape, F32),
        ],
        scratch_shapes=[pltpu.VMEM((t * batch_block, d_conv + d_pool), BF16)],
        compiler_params=pltpu.CompilerParams(
            dimension_semantics=("arbitrary",), vmem_limit_bytes=VMEM_LIMIT_BYTES),
        name="mixer_sample",
    )(x, mod4, gain, w_in, w_dw, b_dw, ln_g, ln_b, w_pool, pool_scale, w_out, state_conv, state_pool)


def kernel(x_prompt, x_sample, state_conv, state_pool, c_prompt, c_sample, w_ada, b_ada, g_ffn1, w_ffn1_up, w_ffn1_down, g_mix, w_in, w_dw, b_dw, ln_g, ln_b, w_pool, pool_scale, w_out, g_ffn2, w_ffn2_up, w_ffn2_down, g_final):
    depth = w_ada.shape[0]
    assert depth == 1 and state_conv.shape[0] == 1 and state_pool.shape[0] == 1
    bp, tp, d = x_prompt.shape
    bs, ts, _ = x_sample.shape

    mod_p, mod_s = _adaln(c_prompt, c_sample, w_ada[0], b_ada[0])
    mod_p = mod_p.reshape(N_MOD, bp, 1, d)
    mod_s = mod_s.reshape(N_MOD, 1, bs, d)

    bf = lambda w: w[0].astype(BF16)
    w1u, w1d, w2u, w2d = bf(w_ffn1_up), bf(w_ffn1_down), bf(w_ffn2_up), bf(w_ffn2_down)
    mixer_w = (g_mix, bf(w_in), w_dw[0], b_dw, ln_g, ln_b, bf(w_pool), pool_scale, bf(w_out))
    g_final2 = g_final.reshape(1, d)

    tile_p = min(tp, 1024)
    grid_p = (bp, tp // tile_p)
    x_spec_p = pl.BlockSpec((1, tile_p, d), lambda i, j: (i, j, 0))
    mod_spec_p = lambda mb: pl.BlockSpec((3, 1, 1, d), lambda i, j: (mb, i, 0, 0))
    seq_s = min(bs, 64)
    grid_s = (bs // seq_s,)
    x_spec_s = pl.BlockSpec((ts, seq_s, d), lambda i: (0, i, 0))
    mod_spec_s = lambda mb: pl.BlockSpec((3, 1, seq_s, d), lambda i: (mb, 0, i, 0))
    xs = jnp.transpose(x_sample, (1, 0, 2))
    sconv = jnp.transpose(state_conv, (0, 2, 1, 3))
    spool = jnp.transpose(state_pool, (0, 2, 1, 3))

    h_p = _ffn(x_prompt, x_spec_p, mod_p, mod_spec_p(0), grid_p, g_ffn1, w1u, w1d, None)
    h_s = _ffn(xs, x_spec_s, mod_s, mod_spec_s(0), grid_s, g_ffn1, w1u, w1d, None)

    h_p, conv_p, pool_p = _mixer_prompt(h_p, mod_p, *mixer_w, time_block=min(tp, 512))
    h_s, conv_s, pool_s = _mixer_sample(h_s, mod_s, *mixer_w, sconv, spool, batch_block=min(bs, 32))

    y_p = _ffn(h_p, x_spec_p, mod_p, mod_spec_p(2), grid_p, g_ffn2, w2u, w2d, g_final2)
    y_s = _ffn(h_s, x_spec_s, mod_s, mod_spec_s(2), grid_s, g_ffn2, w2u, w2d, g_final2)
    return (y_p, jnp.transpose(y_s, (1, 0, 2)), conv_p, pool_p,
            jnp.transpose(conv_s, (0, 2, 1, 3)), jnp.transpose(pool_s, (0, 2, 1, 3)))
```

```python
import functools

import jax
import jax.numpy as jnp
from jax import lax
from jax.experimental import pallas as pl
from jax.experimental.pallas import tpu as pltpu

EPS = 1e-6
HALF = 0.5
N_MOD = 9
CONV_WIDTH = 31
POOL_WINDOWS = (2, 4, 8, 16)
POOL_MAX = max(POOL_WINDOWS)
SAMPLE_POS0 = 16384

LANES = 128
BF16_SUBLANES = 16
CONV_HALO = 32
POOL_HALO = 16
VMEM_LIMIT_BYTES = 56 * 1024 * 1024

F32 = jnp.float32
BF16 = jnp.bfloat16


def _dot(a, b):
    return jnp.dot(a, b, preferred_element_type=F32)


def _sigmoid(x):
    return 1.0 / (1.0 + jnp.exp(-x))


def _rms_scale(x):
    return lax.rsqrt(jnp.mean(x * x, axis=-1, keepdims=True) + EPS)


def _const_spec(shape):
    zeros = (0,) * len(shape)
    return pl.BlockSpec(shape, lambda *_: zeros, pipeline_mode=pl.Buffered(1))


def _adaln_kernel(cp_ref, cs_ref, w_ref, b_ref, op_ref, os_ref):
    w = w_ref[...].astype(BF16)
    b = b_ref[0]

    def mod(c):
        s = c * _sigmoid(c)
        return _dot(s.astype(BF16), w) + b

    op_ref[0] = mod(cp_ref[...])
    os_ref[0] = mod(cs_ref[...])


def _adaln(c_prompt, c_sample, w_ada, b_ada):
    bp, d = c_prompt.shape
    bs = c_sample.shape[0]
    return pl.pallas_call(
        _adaln_kernel,
        grid=(N_MOD,),
        in_specs=[
            pl.BlockSpec((bp, d), lambda j: (0, 0)),
            pl.BlockSpec((bs, d), lambda j: (0, 0)),
            pl.BlockSpec((d, d), lambda j: (0, j)),
            pl.BlockSpec((1, 1, d), lambda j: (j, 0, 0)),
        ],
        out_specs=[
            pl.BlockSpec((1, bp, d), lambda j: (j, 0, 0)),
            pl.BlockSpec((1, bs, d), lambda j: (j, 0, 0)),
        ],
        out_shape=[
            jax.ShapeDtypeStruct((N_MOD, bp, d), F32),
            jax.ShapeDtypeStruct((N_MOD, bs, d), F32),
        ],
        compiler_params=pltpu.CompilerParams(
            dimension_semantics=("arbitrary",), vmem_limit_bytes=VMEM_LIMIT_BYTES),
        name="adaln",
    )(c_prompt, c_sample, w_ada, b_ada.reshape(N_MOD, 1, d))


def _ffn_in(x, mod, gain, xb_scr):
    shift, scale, _ = mod
    xn = (x * _rms_scale(x)) * gain * (1.0 + scale) + shift
    xb_scr[...] = xn.reshape(xb_scr.shape).astype(BF16)


def _ffn_chunk(lo, ff_chunk, wup_ref, xb_scr, a_scr):
    d_ff = a_scr.shape[1]
    xb = xb_scr[...]
    g = _dot(xb, wup_ref[:, lo:lo + ff_chunk])
    u = _dot(xb, wup_ref[:, d_ff + lo:d_ff + lo + ff_chunk])
    a_scr[:, lo:lo + ff_chunk] = (g * _sigmoid(g) * u).astype(BF16)


def _ffn_out(x, mod, wdn_ref, a_scr, g_final):
    gate = mod[2]
    out = x + (HALF * gate) * _dot(a_scr[...], wdn_ref[...]).reshape(x.shape)
    if g_final is not None:
        out = (out * _rms_scale(out)) * g_final
    return out


def _ffn_kernel(x_ref, mod_ref, g_ref, wup_ref, wdn_ref, *rest, ff_chunk, final_norm, n_casts):
    rest = list(rest)
    gf_ref = rest.pop(0) if final_norm else None
    cast_in = [rest.pop(0) for _ in range(n_casts)]
    o_ref = rest.pop(0)
    cast_out = [rest.pop(0) for _ in range(n_casts)]
    xb_scr, a_scr = rest
    mod = (mod_ref[0], mod_ref[1], mod_ref[2])
    _ffn_in(x_ref[...], mod, g_ref[...], xb_scr)
    for lo in range(0, a_scr.shape[1], ff_chunk):
        _ffn_chunk(lo, ff_chunk, wup_ref, xb_scr, a_scr)
    o_ref[...] = _ffn_out(x_ref[...], mod, wdn_ref, a_scr, gf_ref[...] if final_norm else None)
    for src, dst in zip(cast_in, cast_out):
        dst[...] = src[...].astype(BF16)


def _ffn(x, x_spec, mod4, mod_spec, grid, gain, w_up, w_down, g_final, *, ff_chunk=256, casts=()):
    d = x.shape[-1]
    d_ff = w_down.shape[0]
    m = x_spec.block_shape[0] * x_spec.block_shape[1]
    final_norm = g_final is not None
    in_specs = [x_spec, mod_spec, _const_spec((1, d)), _const_spec((d, 2 * d_ff)), _const_spec((d_ff, d))]
    args = [x, mod4, gain, w_up, w_down]
    if final_norm:
        in_specs.append(_const_spec((1, d)))
        args.append(g_final)
    n_steps = 1
    for g in grid:
        n_steps *= g

    def flat_step(*idx):
        s = idx[0]
        for i, g in zip(idx[1:], grid[1:]):
            s = s * g + i
        return s

    cast_specs = []
    for w in casts:
        rows, cols = w.shape
        assert rows % (n_steps * BF16_SUBLANES) == 0, (w.shape, n_steps)
        cast_specs.append(pl.BlockSpec((rows // n_steps, cols), lambda *idx: (flat_step(*idx), 0)))
    outs = pl.pallas_call(
        functools.partial(_ffn_kernel, ff_chunk=ff_chunk, final_norm=final_norm, n_casts=len(casts)),
        grid=grid,
        in_specs=in_specs + cast_specs,
        out_specs=[x_spec] + cast_specs,
        out_shape=[jax.ShapeDtypeStruct(x.shape, F32)] + [jax.ShapeDtypeStruct(w.shape, BF16) for w in casts],
        scratch_shapes=[pltpu.VMEM((m, d), BF16), pltpu.VMEM((m, d_ff), BF16)],
        compiler_params=pltpu.CompilerParams(
            dimension_semantics=("arbitrary",) * len(grid), vmem_limit_bytes=VMEM_LIMIT_BYTES),
        name="ffn_final" if final_norm else "ffn",
    )(*args, *casts)
    return outs[0], outs[1:]


def _layernorm_silu(v, ln_g, ln_b):
    mu = jnp.mean(v, axis=-1, keepdims=True)
    c = v - mu
    var = jnp.mean(c * c, axis=-1, keepdims=True)
    y = (c * lax.rsqrt(var + EPS)) * ln_g + ln_b
    return y * _sigmoid(y)


def _mixer_rows_in(x, mod, gain, win_ref, cs, ps, r0):
    sb = x.shape[0]
    n_slabs = cs.shape[0]
    d_conv = n_slabs * LANES
    shift, scale, _ = mod
    hm = (x * _rms_scale(x)) * gain * (1.0 + scale) + shift
    proj = _dot(hm.astype(BF16), win_ref[...])
    u_conv = proj[:, :d_conv] * _sigmoid(proj[:, d_conv:2 * d_conv])
    for s in range(n_slabs):
        cs[s, CONV_HALO + r0:CONV_HALO + r0 + sb, :] = u_conv[:, s * LANES:(s + 1) * LANES]
        ps[s, POOL_HALO + r0:POOL_HALO + r0 + sb, :] = (
            proj[:, 2 * d_conv + s * LANES:2 * d_conv + (s + 1) * LANES])


def _mixer_rows_out(x, mod, pos0, r0, row_block, wdw_ref, bdw_ref, lng_ref, lnb_ref,
                    wpool_ref, pscale_ref, wout_ref, cs, ps, v_scr, mix_scr):
    sb = x.shape[0]
    n_slabs = cs.shape[0]
    d_conv = n_slabs * LANES
    assert len(POOL_WINDOWS) == ps.shape[0] and wpool_ref.shape[1] == LANES
    gate = mod[2]
    rows = slice(r0, r0 + sb)

    first = CONV_HALO - (CONV_WIDTH - 1)
    for s in range(n_slabs):
        lanes = slice(s * LANES, (s + 1) * LANES)
        for r in range(r0, r0 + sb, row_block):
            acc = jnp.broadcast_to(bdw_ref[:, lanes], (row_block, LANES))
            for k in range(CONV_WIDTH):
                acc = acc + wdw_ref[k:k + 1, lanes] * cs[s, first + k + r:first + k + r + row_block, :]
            v_scr[r:r + row_block, lanes] = acc
    mix_scr[rows, 0:d_conv] = _layernorm_silu(v_scr[rows, :], lng_ref[...], lnb_ref[...]).astype(BF16)

    pos = pos0 + r0 + lax.broadcasted_iota(jnp.int32, (sb, 1), 0)
    for gi, w in enumerate(POOL_WINDOWS):
        lanes = slice(gi * LANES, (gi + 1) * LANES)
        cur = ps[gi, POOL_HALO + r0:POOL_HALO + r0 + sb, :]
        win_sum = cur
        for j in range(1, w):
            win_sum = win_sum + ps[gi, POOL_HALO + r0 - j:POOL_HALO + r0 - j + sb, :]
        count = jnp.minimum(pos + 1, w).astype(F32)
        pooled = win_sum / count - cur
        mixed = _dot(pooled.astype(BF16), wpool_ref[gi])
        mix_scr[rows, d_conv + gi * LANES:d_conv + (gi + 1) * LANES] = (
            mixed * pscale_ref[:, lanes]).astype(BF16)

    return x + gate * _dot(mix_scr[rows, :], wout_ref[...])


def _mixer_prompt_kernel(x_ref, mod_ref, g_ref, win_ref, wdw_ref, bdw_ref, lng_ref, lnb_ref,
                         wpool_ref, pscale_ref, wout_ref,
                         o_ref, oconv_ref, opool_ref,
                         cs, ps, v_scr, mix_scr, *, sub_block, row_block):
    t_idx = pl.program_id(1)
    n_t = pl.num_programs(1)
    tm = x_ref.shape[1]
    n_slabs = cs.shape[0]

    @pl.when(t_idx == 0)
    def _():
        cs[:, 0:CONV_HALO, :] = jnp.zeros((n_slabs, CONV_HALO, LANES), F32)
        ps[:, 0:POOL_HALO, :] = jnp.zeros((ps.shape[0], POOL_HALO, LANES), F32)

    mod = (mod_ref[0, 0], mod_ref[1, 0], mod_ref[2, 0])
    for r0 in range(0, tm, sub_block):
        x = x_ref[0, r0:r0 + sub_block, :]
        _mixer_rows_in(x, mod, g_ref[...], win_ref, cs, ps, r0)
        o_ref[0, r0:r0 + sub_block, :] = _mixer_rows_out(
            x, mod, t_idx * tm, r0, row_block, wdw_ref, bdw_ref,
            lng_ref, lnb_ref, wpool_ref, pscale_ref, wout_ref, cs, ps, v_scr, mix_scr)

    @pl.when(t_idx == n_t - 1)
    def _():
        for s in range(n_slabs):
            lanes = slice(s * LANES, (s + 1) * LANES)
            oconv_ref[0, 0, :, lanes] = cs[s, CONV_HALO + tm - (CONV_WIDTH - 1):CONV_HALO + tm, :]
            opool_ref[0, 0, :, lanes] = ps[s, POOL_HALO + tm - (POOL_MAX - 1):POOL_HALO + tm, :]

    cs[:, 0:CONV_HALO, :] = cs[:, tm:tm + CONV_HALO, :]
    ps[:, 0:POOL_HALO, :] = ps[:, tm:tm + POOL_HALO, :]


def _mixer_prompt(x, mod4, gain, w_in, w_dw, b_dw, ln_g, ln_b, w_pool, pool_scale, w_out,
                  *, time_block, sub_block, row_block=64):
    b, t, d = x.shape
    d_conv = w_dw.shape[1]
    d_pool = pool_scale.shape[1]
    n_groups, group = w_pool.shape[0], w_pool.shape[1]
    assert group == LANES and d_conv % LANES == 0 and d_pool == n_groups * LANES
    assert time_block % sub_block == 0 and sub_block % row_block == 0
    x_spec = pl.BlockSpec((1, time_block, d), lambda i, j: (i, j, 0))
    return pl.pallas_call(
        functools.partial(_mixer_prompt_kernel, sub_block=sub_block, row_block=row_block),
        grid=(b, t // time_block),
        in_specs=[
            x_spec,
            pl.BlockSpec((3, 1, 1, d), lambda i, j: (1, i, 0, 0)),
            _const_spec((1, d)),
            _const_spec((d, 2 * d_conv + d_pool)),
            _const_spec((CONV_WIDTH, d_conv)),
            _const_spec((1, d_conv)),
            _const_spec((1, d_conv)),
            _const_spec((1, d_conv)),
            _const_spec((n_groups, group, group)),
            _const_spec((1, d_pool)),
            _const_spec((d_conv + d_pool, d)),
        ],
        out_specs=[
            x_spec,
            pl.BlockSpec((1, 1, CONV_WIDTH - 1, d_conv), lambda i, j: (0, i, 0, 0)),
            pl.BlockSpec((1, 1, POOL_MAX - 1, d_pool), lambda i, j: (0, i, 0, 0)),
        ],
        out_shape=[
            jax.ShapeDtypeStruct(x.shape, F32),
            jax.ShapeDtypeStruct((1, b, CONV_WIDTH - 1, d_conv), F32),
            jax.ShapeDtypeStruct((1, b, POOL_MAX - 1, d_pool), F32),
        ],
        scratch_shapes=[
            pltpu.VMEM((d_conv // LANES, CONV_HALO + time_block, LANES), F32),
            pltpu.VMEM((n_groups, POOL_HALO + time_block, LANES), F32),
            pltpu.VMEM((time_block, d_conv), F32),
            pltpu.VMEM((time_block, d_conv + d_pool), BF16),
        ],
        compiler_params=pltpu.CompilerParams(
            dimension_semantics=("arbitrary", "arbitrary"), vmem_limit_bytes=VMEM_LIMIT_BYTES),
        name="mixer_prompt",
    )(x, mod4, gain, w_in, w_dw, b_dw, ln_g, ln_b, w_pool, pool_scale, w_out)


def _mixer_sample_kernel(x_ref, mod_ref, g_ref, win_ref, wdw_ref, bdw_ref, lng_ref, lnb_ref,
                         wpool_ref, pscale_ref, wout_ref, sconv_ref, spool_ref,
                         o_ref, oconv_ref, opool_ref, mix_scr):
    tt, bb, d = x_ref.shape
    m = tt * bb
    n_conv, d_conv = sconv_ref.shape[1], sconv_ref.shape[3]
    n_pool = spool_ref.shape[1]

    x = x_ref[...]
    shift, scale, gate = mod_ref[0], mod_ref[1], mod_ref[2]
    hm = (x * _rms_scale(x)) * g_ref[...] * (1.0 + scale) + shift
    proj = _dot(hm.reshape(m, d).astype(BF16), win_ref[...])
    u_conv = proj[:, :d_conv] * _sigmoid(proj[:, d_conv:2 * d_conv])
    u_pool = proj[:, 2 * d_conv:]

    def conv_in(s):
        return sconv_ref[0, s] if s < n_conv else u_conv[(s - n_conv) * bb:(s - n_conv + 1) * bb, :]

    def pool_in(s, lanes=slice(None)):
        return spool_ref[0, s, :, lanes] if s < n_pool else u_pool[(s - n_pool) * bb:(s - n_pool + 1) * bb, lanes]

    for s in range(n_conv):
        oconv_ref[0, s] = conv_in(s + tt)
    for s in range(n_pool):
        opool_ref[0, s] = pool_in(s + tt)

    for t in range(tt):
        acc = jnp.broadcast_to(bdw_ref[...], (bb, d_conv))
        for k in range(CONV_WIDTH):
            acc = acc + wdw_ref[k:k + 1, :] * conv_in(t + k)
        mix_scr[t * bb:(t + 1) * bb, 0:d_conv] = _layernorm_silu(acc, lng_ref[...], lnb_ref[...]).astype(BF16)

    for gi, w in enumerate(POOL_WINDOWS):
        lanes = slice(gi * LANES, (gi + 1) * LANES)
        count = float(min(SAMPLE_POS0 + 1, w))
        for t in range(tt):
            cur = pool_in(n_pool + t, lanes)
            win_sum = cur
            for j in range(1, w):
                win_sum = win_sum + pool_in(n_pool + t - j, lanes)
            pooled = win_sum / count - cur
            mixed = _dot(pooled.astype(BF16), wpool_ref[gi])
            mix_scr[t * bb:(t + 1) * bb, d_conv + gi * LANES:d_conv + (gi + 1) * LANES] = (
                mixed * pscale_ref[:, lanes]).astype(BF16)

    o_ref[...] = x + gate * _dot(mix_scr[...], wout_ref[...]).reshape(tt, bb, d)


def _mixer_sample(x, mod4, gain, w_in, w_dw, b_dw, ln_g, ln_b, w_pool, pool_scale, w_out,
                  state_conv, state_pool, *, batch_block):
    t, b, d = x.shape
    d_conv = w_dw.shape[1]
    d_pool = pool_scale.shape[1]
    n_groups, group = w_pool.shape[0], w_pool.shape[1]
    assert group == LANES and d_pool == n_groups * LANES and d_conv == d_pool
    x_spec = pl.BlockSpec((t, batch_block, d), lambda i: (0, i, 0))
    conv_spec = pl.BlockSpec((1, CONV_WIDTH - 1, batch_block, d_conv), lambda i: (0, 0, i, 0))
    pool_spec = pl.BlockSpec((1, POOL_MAX - 1, batch_block, d_pool), lambda i: (0, 0, i, 0))
    return pl.pallas_call(
        _mixer_sample_kernel,
        grid=(b // batch_block,),
        in_specs=[
            x_spec,
            pl.BlockSpec((3, 1, batch_block, d), lambda i: (1, 0, i, 0)),
            _const_spec((1, d)),
            _const_spec((d, 2 * d_conv + d_pool)),
            _const_spec((CONV_WIDTH, d_conv)),
            _const_spec((1, d_conv)),
            _const_spec((1, d_conv)),
            _const_spec((1, d_conv)),
            _const_spec((n_groups, group, group)),
            _const_spec((1, d_pool)),
            _const_spec((d_conv + d_pool, d)),
            conv_spec,
            pool_spec,
        ],
        out_specs=[x_spec, conv_spec, pool_spec],
        out_shape=[
            jax.ShapeDtypeStruct(x.shape, F32),
            jax.ShapeDtypeStruct(state_conv.shape, F32),
            jax.ShapeDtypeStruct(state_pool.shape, F32),
        ],
        scratch_shapes=[pltpu.VMEM((t * batch_block, d_conv + d_pool), BF16)],
        compiler_params=pltpu.CompilerParams(
            dimension_semantics=("arbitrary",), vmem_limit_bytes=VMEM_LIMIT_BYTES),
        name="mixer_sample",
    )(x, mod4, gain, w_in, w_dw, b_dw, ln_g, ln_b, w_pool, pool_scale, w_out, state_conv, state_pool)


def kernel(x_prompt, x_sample, state_conv, state_pool, c_prompt, c_sample, w_ada, b_ada, g_ffn1, w_ffn1_up, w_ffn1_down, g_mix, w_in, w_dw, b_dw, ln_g, ln_b, w_pool, pool_scale, w_out, g_ffn2, w_ffn2_up, w_ffn2_down, g_final):
    depth = w_ada.shape[0]
    assert depth == 1 and state_conv.shape[0] == 1 and state_pool.shape[0] == 1
    bp, tp, d = x_prompt.shape
    bs, ts, _ = x_sample.shape

    mod_p, mod_s = _adaln(c_prompt, c_sample, w_ada[0], b_ada[0])
    mod_p = mod_p.reshape(N_MOD, bp, 1, d)
    mod_s = mod_s.reshape(N_MOD, 1, bs, d)

    w1u, w1d = w_ffn1_up[0].astype(BF16), w_ffn1_down[0].astype(BF16)
    g_final2 = g_final.reshape(1, d)
    n_groups, group, _ = w_pool.shape[1:]

    tile_p = min(tp, 1024)
    grid_p = (bp, tp // tile_p)
    x_spec_p = pl.BlockSpec((1, tile_p, d), lambda i, j: (i, j, 0))
    mod_spec_p = lambda mb: pl.BlockSpec((3, 1, 1, d), lambda i, j: (mb, i, 0, 0))
    seq_s = min(bs, 64)
    grid_s = (bs // seq_s,)
    x_spec_s = pl.BlockSpec((ts, seq_s, d), lambda i: (0, i, 0))
    mod_spec_s = lambda mb: pl.BlockSpec((3, 1, seq_s, d), lambda i: (mb, 0, i, 0))
    xs = jnp.transpose(x_sample, (1, 0, 2))
    sconv = jnp.transpose(state_conv, (0, 2, 1, 3))
    spool = jnp.transpose(state_pool, (0, 2, 1, 3))

    later = (w_ffn2_up[0], w_ffn2_down[0], w_in[0], w_out[0], w_pool[0].reshape(n_groups * group, group))
    h_p, (w2u, w2d, w_in_b, w_out_b, w_pool_b) = _ffn(
        x_prompt, x_spec_p, mod_p, mod_spec_p(0), grid_p, g_ffn1, w1u, w1d, None, casts=later)
    mixer_w = (g_mix, w_in_b, w_dw[0], b_dw, ln_g, ln_b, w_pool_b.reshape(n_groups, group, group),
               pool_scale, w_out_b)
    h_s, _ = _ffn(xs, x_spec_s, mod_s, mod_spec_s(0), grid_s, g_ffn1, w1u, w1d, None)

    h_p, conv_p, pool_p = _mixer_prompt(h_p, mod_p, *mixer_w, time_block=tile_p, sub_block=min(tile_p, 256))
    h_s, conv_s, pool_s = _mixer_sample(h_s, mod_s, *mixer_w, sconv, spool, batch_block=min(bs, 32))

    y_p, _ = _ffn(h_p, x_spec_p, mod_p, mod_spec_p(2), grid_p, g_ffn2, w2u, w2d, g_final2)
    y_s, _ = _ffn(h_s, x_spec_s, mod_s, mod_spec_s(2), grid_s, g_ffn2, w2u, w2d, g_final2)
    return (y_p, jnp.transpose(y_s, (1, 0, 2)), conv_p, pool_p,
            jnp.transpose(conv_s, (0, 2, 1, 3)), jnp.transpose(pool_s, (0, 2, 1, 3)))
```

```python
import functools

import jax
import jax.numpy as jnp
from jax import lax
from jax.experimental import pallas as pl
from jax.experimental.pallas import tpu as pltpu

EPS = 1e-6
HALF = 0.5
N_MOD = 9
CONV_WIDTH = 31
POOL_WINDOWS = (2, 4, 8, 16)
POOL_MAX = max(POOL_WINDOWS)
SAMPLE_POS0 = 16384

LANES = 128
BF16_SUBLANES = 16
FIRST_CHUNK_SLABS = 4
CONV_HALO = 32
POOL_HALO = 16
VMEM_LIMIT_BYTES = 56 * 1024 * 1024

F32 = jnp.float32
BF16 = jnp.bfloat16


def _dot(a, b):
    return jnp.dot(a, b, preferred_element_type=F32)


def _sigmoid(x):
    return 1.0 / (1.0 + jnp.exp(-x))


def _rms_scale(x):
    return lax.rsqrt(jnp.mean(x * x, axis=-1, keepdims=True) + EPS)


def _const_spec(shape):
    zeros = (0,) * len(shape)
    return pl.BlockSpec(shape, lambda *_: zeros, pipeline_mode=pl.Buffered(1))


def _cast_specs(casts, n_slabs, slab_of_step):
    specs = []
    for w in casts:
        rows, cols = w.shape
        assert rows % (n_slabs * BF16_SUBLANES) == 0, (w.shape, n_slabs)
        specs.append(pl.BlockSpec((rows // n_slabs, cols), lambda *idx: (slab_of_step(*idx), 0)))
    return specs


def _cast_slabs(cast_in, cast_out):
    for src, dst in zip(cast_in, cast_out):
        dst[...] = src[...].astype(BF16)


def _adaln_kernel(cp_ref, cs_ref, w_ref, b_ref, *rest, n_casts):
    cast_in, (op_ref, os_ref), cast_out = rest[:n_casts], rest[n_casts:n_casts + 2], rest[n_casts + 2:]
    w = w_ref[...].astype(BF16)
    b = b_ref[0]

    def mod(c):
        s = c * _sigmoid(c)
        return _dot(s.astype(BF16), w) + b

    op_ref[0] = mod(cp_ref[...])
    os_ref[0] = mod(cs_ref[...])
    _cast_slabs(cast_in, cast_out)


def _adaln(c_prompt, c_sample, w_ada, b_ada, *, col_block, cast_slabs, casts=()):
    bp, d = c_prompt.shape
    bs = c_sample.shape[0]
    per_mod = d // col_block
    n_steps = N_MOD * per_mod
    assert d % col_block == 0 and n_steps >= cast_slabs
    cast_specs = _cast_specs(casts, cast_slabs, lambda j: jnp.minimum(j, cast_slabs - 1))
    outs = pl.pallas_call(
        functools.partial(_adaln_kernel, n_casts=len(casts)),
        grid=(n_steps,),
        in_specs=[
            pl.BlockSpec((bp, d), lambda j: (0, 0)),
            pl.BlockSpec((bs, d), lambda j: (0, 0)),
            pl.BlockSpec((d, col_block), lambda j: (0, j)),
            pl.BlockSpec((1, 1, col_block), lambda j: (j, 0, 0)),
        ] + cast_specs,
        out_specs=[
            pl.BlockSpec((1, bp, col_block), lambda j: (j // per_mod, 0, j % per_mod)),
            pl.BlockSpec((1, bs, col_block), lambda j: (j // per_mod, 0, j % per_mod)),
        ] + cast_specs,
        out_shape=[
            jax.ShapeDtypeStruct((N_MOD, bp, d), F32),
            jax.ShapeDtypeStruct((N_MOD, bs, d), F32),
        ] + [jax.ShapeDtypeStruct(w.shape, BF16) for w in casts],
        compiler_params=pltpu.CompilerParams(
            dimension_semantics=("arbitrary",), vmem_limit_bytes=VMEM_LIMIT_BYTES),
        name="adaln",
    )(c_prompt, c_sample, w_ada, b_ada.reshape(n_steps, 1, col_block), *casts)
    return outs[0], outs[1], outs[2:]


def _ffn_in(x, mod, gain, xb_scr, r0=0):
    shift, scale, _ = mod
    xn = (x * _rms_scale(x)) * gain * (1.0 + scale) + shift
    m = xn.size // xn.shape[-1]
    xb_scr[r0:r0 + m, :] = xn.reshape(m, xn.shape[-1]).astype(BF16)


def _ffn_chunk(lo, ff_chunk, wup_ref, xb_scr, a_scr, rows=slice(None)):
    d_ff = a_scr.shape[1]
    xb = xb_scr[rows, :]
    g = _dot(xb, wup_ref[:, lo:lo + ff_chunk])
    u = _dot(xb, wup_ref[:, d_ff + lo:d_ff + lo + ff_chunk])
    a_scr[rows, lo:lo + ff_chunk] = (g * _sigmoid(g) * u).astype(BF16)


def _ffn_out(x, mod, wdn_ref, a_scr, g_final):
    gate = mod[2]
    out = x + (HALF * gate) * _dot(a_scr[...], wdn_ref[...]).reshape(x.shape)
    if g_final is not None:
        out = (out * _rms_scale(out)) * g_final
    return out


def _ffn_kernel(x_ref, mod_ref, g_ref, wup_ref, wdn_ref, *rest, ff_chunk, final_norm, n_casts):
    rest = list(rest)
    gf_ref = rest.pop(0) if final_norm else None
    cast_in = [rest.pop(0) for _ in range(n_casts)]
    o_ref = rest.pop(0)
    cast_out = [rest.pop(0) for _ in range(n_casts)]
    xb_scr, a_scr = rest
    mod = (mod_ref[0], mod_ref[1], mod_ref[2])
    lead, rows, _ = x_ref.shape
    if lead == 1 and rows % (FIRST_CHUNK_SLABS * BF16_SUBLANES) == 0:
        slab = rows // FIRST_CHUNK_SLABS
        for r0 in range(0, rows, slab):
            _ffn_in(x_ref[:, r0:r0 + slab, :], mod, g_ref[...], xb_scr, r0)
            _ffn_chunk(0, ff_chunk, wup_ref, xb_scr, a_scr, slice(r0, r0 + slab))
        first = ff_chunk
    else:
        _ffn_in(x_ref[...], mod, g_ref[...], xb_scr)
        first = 0
    for lo in range(first, a_scr.shape[1], ff_chunk):
        _ffn_chunk(lo, ff_chunk, wup_ref, xb_scr, a_scr)
    o_ref[...] = _ffn_out(x_ref[...], mod, wdn_ref, a_scr, gf_ref[...] if final_norm else None)
    _cast_slabs(cast_in, cast_out)


def _ffn(x, x_spec, mod4, mod_spec, grid, gain, w_up, w_down, g_final, *, ff_chunk=256, casts=()):
    d = x.shape[-1]
    d_ff = w_down.shape[0]
    m = x_spec.block_shape[0] * x_spec.block_shape[1]
    final_norm = g_final is not None
    in_specs = [x_spec, mod_spec, _const_spec((1, d)), _const_spec((d, 2 * d_ff)), _const_spec((d_ff, d))]
    args = [x, mod4, gain, w_up, w_down]
    if final_norm:
        in_specs.append(_const_spec((1, d)))
        args.append(g_final)
    n_steps = 1
    for g in grid:
        n_steps *= g

    def flat_step(*idx):
        s = idx[0]
        for i, g in zip(idx[1:], grid[1:]):
            s = s * g + i
        return s

    cast_specs = _cast_specs(casts, n_steps, flat_step)
    outs = pl.pallas_call(
        functools.partial(_ffn_kernel, ff_chunk=ff_chunk, final_norm=final_norm, n_casts=len(casts)),
        grid=grid,
        in_specs=in_specs + cast_specs,
        out_specs=[x_spec] + cast_specs,
        out_shape=[jax.ShapeDtypeStruct(x.shape, F32)] + [jax.ShapeDtypeStruct(w.shape, BF16) for w in casts],
        scratch_shapes=[pltpu.VMEM((m, d), BF16), pltpu.VMEM((m, d_ff), BF16)],
        compiler_params=pltpu.CompilerParams(
            dimension_semantics=("arbitrary",) * len(grid), vmem_limit_bytes=VMEM_LIMIT_BYTES),
        name="ffn_final" if final_norm else "ffn",
    )(*args, *casts)
    return outs[0], outs[1:]


def _layernorm_silu(v, ln_g, ln_b):
    mu = jnp.mean(v, axis=-1, keepdims=True)
    c = v - mu
    var = jnp.mean(c * c, axis=-1, keepdims=True)
    y = (c * lax.rsqrt(var + EPS)) * ln_g + ln_b
    return y * _sigmoid(y)


def _mixer_rows_in(x, mod, gain, win_ref, cs, ps, r0):
    sb = x.shape[0]
    n_slabs = cs.shape[0]
    d_conv = n_slabs * LANES
    shift, scale, _ = mod
    hm = (x * _rms_scale(x)) * gain * (1.0 + scale) + shift
    proj = _dot(hm.astype(BF16), win_ref[...])
    u_conv = proj[:, :d_conv] * _sigmoid(proj[:, d_conv:2 * d_conv])
    for s in range(n_slabs):
        cs[s, CONV_HALO + r0:CONV_HALO + r0 + sb, :] = u_conv[:, s * LANES:(s + 1) * LANES]
        ps[s, POOL_HALO + r0:POOL_HALO + r0 + sb, :] = (
            proj[:, 2 * d_conv + s * LANES:2 * d_conv + (s + 1) * LANES])


def _mixer_rows_out(x, mod, pos0, r0, row_block, wdw_ref, bdw_ref, lng_ref, lnb_ref,
                    wpool_ref, pscale_ref, wout_ref, cs, ps, v_scr, mix_scr):
    sb = x.shape[0]
    n_slabs = cs.shape[0]
    d_conv = n_slabs * LANES
    assert len(POOL_WINDOWS) == ps.shape[0] and wpool_ref.shape[1] == LANES
    gate = mod[2]
    rows = slice(r0, r0 + sb)

    first = CONV_HALO - (CONV_WIDTH - 1)
    for s in range(n_slabs):
        lanes = slice(s * LANES, (s + 1) * LANES)
        for r in range(r0, r0 + sb, row_block):
            acc = jnp.broadcast_to(bdw_ref[:, lanes], (row_block, LANES))
            for k in range(CONV_WIDTH):
                acc = acc + wdw_ref[k:k + 1, lanes] * cs[s, first + k + r:first + k + r + row_block, :]
            v_scr[r:r + row_block, lanes] = acc
    mix_scr[rows, 0:d_conv] = _layernorm_silu(v_scr[rows, :], lng_ref[...], lnb_ref[...]).astype(BF16)

    pos = pos0 + r0 + lax.broadcasted_iota(jnp.int32, (sb, 1), 0)
    for gi, w in enumerate(POOL_WINDOWS):
        lanes = slice(gi * LANES, (gi + 1) * LANES)
        cur = ps[gi, POOL_HALO + r0:POOL_HALO + r0 + sb, :]
        win_sum = cur
        for j in range(1, w):
            win_sum = win_sum + ps[gi, POOL_HALO + r0 - j:POOL_HALO + r0 - j + sb, :]
        count = jnp.minimum(pos + 1, w).astype(F32)
        pooled = win_sum / count - cur
        mixed = _dot(pooled.astype(BF16), wpool_ref[gi])
        mix_scr[rows, d_conv + gi * LANES:d_conv + (gi + 1) * LANES] = (
            mixed * pscale_ref[:, lanes]).astype(BF16)

    return x + gate * _dot(mix_scr[rows, :], wout_ref[...])


def _mixer_prompt_kernel(x_ref, mod_ref, g_ref, win_ref, wdw_ref, bdw_ref, lng_ref, lnb_ref,
                         wpool_ref, pscale_ref, wout_ref,
                         o_ref, oconv_ref, opool_ref,
                         cs, ps, v_scr, mix_scr, *, sub_block, row_block):
    t_idx = pl.program_id(1)
    n_t = pl.num_programs(1)
    tm = x_ref.shape[1]
    n_slabs = cs.shape[0]

    @pl.when(t_idx == 0)
    def _():
        cs[:, 0:CONV_HALO, :] = jnp.zeros((n_slabs, CONV_HALO, LANES), F32)
        ps[:, 0:POOL_HALO, :] = jnp.zeros((ps.shape[0], POOL_HALO, LANES), F32)

    mod = (mod_ref[0, 0], mod_ref[1, 0], mod_ref[2, 0])
    for r0 in range(0, tm, sub_block):
        x = x_ref[0, r0:r0 + sub_block, :]
        _mixer_rows_in(x, mod, g_ref[...], win_ref, cs, ps, r0)
        o_ref[0, r0:r0 + sub_block, :] = _mixer_rows_out(
            x, mod, t_idx * tm, r0, row_block, wdw_ref, bdw_ref,
            lng_ref, lnb_ref, wpool_ref, pscale_ref, wout_ref, cs, ps, v_scr, mix_scr)

    @pl.when(t_idx == n_t - 1)
    def _():
        for s in range(n_slabs):
            lanes = slice(s * LANES, (s + 1) * LANES)
            oconv_ref[0, 0, :, lanes] = cs[s, CONV_HALO + tm - (CONV_WIDTH - 1):CONV_HALO + tm, :]
            opool_ref[0, 0, :, lanes] = ps[s, POOL_HALO + tm - (POOL_MAX - 1):POOL_HALO + tm, :]

    cs[:, 0:CONV_HALO, :] = cs[:, tm:tm + CONV_HALO, :]
    ps[:, 0:POOL_HALO, :] = ps[:, tm:tm + POOL_HALO, :]


def _mixer_prompt(x, mod4, gain, w_in, w_dw, b_dw, ln_g, ln_b, w_pool, pool_scale, w_out,
                  *, time_block, sub_block, row_block=64):
    b, t, d = x.shape
    d_conv = w_dw.shape[1]
    d_pool = pool_scale.shape[1]
    n_groups, group = w_pool.shape[0], w_pool.shape[1]
    assert group == LANES and d_conv % LANES == 0 and d_pool == n_groups * LANES
    assert time_block % sub_block == 0 and sub_block % row_block == 0
    x_spec = pl.BlockSpec((1, time_block, d), lambda i, j: (i, j, 0))
    return pl.pallas_call(
        functools.partial(_mixer_prompt_kernel, sub_block=sub_block, row_block=row_block),
        grid=(b, t // time_block),
        in_specs=[
            x_spec,
            pl.BlockSpec((3, 1, 1, d), lambda i, j: (1, i, 0, 0)),
            _const_spec((1, d)),
            _const_spec((d, 2 * d_conv + d_pool)),
            _const_spec((CONV_WIDTH, d_conv)),
            _const_spec((1, d_conv)),
            _const_spec((1, d_conv)),
            _const_spec((1, d_conv)),
            _const_spec((n_groups, group, group)),
            _const_spec((1, d_pool)),
            _const_spec((d_conv + d_pool, d)),
        ],
        out_specs=[
            x_spec,
            pl.BlockSpec((1, 1, CONV_WIDTH - 1, d_conv), lambda i, j: (0, i, 0, 0)),
            pl.BlockSpec((1, 1, POOL_MAX - 1, d_pool), lambda i, j: (0, i, 0, 0)),
        ],
        out_shape=[
            jax.ShapeDtypeStruct(x.shape, F32),
            jax.ShapeDtypeStruct((1, b, CONV_WIDTH - 1, d_conv), F32),
            jax.ShapeDtypeStruct((1, b, POOL_MAX - 1, d_pool), F32),
        ],
        scratch_shapes=[
            pltpu.VMEM((d_conv // LANES, CONV_HALO + time_block, LANES), F32),
            pltpu.VMEM((n_groups, POOL_HALO + time_block, LANES), F32),
            pltpu.VMEM((time_block, d_conv), F32),
            pltpu.VMEM((time_block, d_conv + d_pool), BF16),
        ],
        compiler_params=pltpu.CompilerParams(
            dimension_semantics=("arbitrary", "arbitrary"), vmem_limit_bytes=VMEM_LIMIT_BYTES),
        name="mixer_prompt",
    )(x, mod4, gain, w_in, w_dw, b_dw, ln_g, ln_b, w_pool, pool_scale, w_out)


def _mixer_sample_kernel(x_ref, mod_ref, g_ref, win_ref, wdw_ref, bdw_ref, lng_ref, lnb_ref,
                         wpool_ref, pscale_ref, wout_ref, sconv_ref, spool_ref,
                         o_ref, oconv_ref, opool_ref, mix_scr):
    tt, bb, d = x_ref.shape
    m = tt * bb
    n_conv, d_conv = sconv_ref.shape[1], sconv_ref.shape[3]
    n_pool = spool_ref.shape[1]

    x = x_ref[...]
    shift, scale, gate = mod_ref[0], mod_ref[1], mod_ref[2]
    hm = (x * _rms_scale(x)) * g_ref[...] * (1.0 + scale) + shift
    proj = _dot(hm.reshape(m, d).astype(BF16), win_ref[...])
    u_conv = proj[:, :d_conv] * _sigmoid(proj[:, d_conv:2 * d_conv])
    u_pool = proj[:, 2 * d_conv:]

    def conv_in(s):
        return sconv_ref[0, s] if s < n_conv else u_conv[(s - n_conv) * bb:(s - n_conv + 1) * bb, :]

    def pool_in(s, lanes=slice(None)):
        return spool_ref[0, s, :, lanes] if s < n_pool else u_pool[(s - n_pool) * bb:(s - n_pool + 1) * bb, lanes]

    for s in range(n_conv):
        oconv_ref[0, s] = conv_in(s + tt)
    for s in range(n_pool):
        opool_ref[0, s] = pool_in(s + tt)

    for t in range(tt):
        acc = jnp.broadcast_to(bdw_ref[...], (bb, d_conv))
        for k in range(CONV_WIDTH):
            acc = acc + wdw_ref[k:k + 1, :] * conv_in(t + k)
        mix_scr[t * bb:(t + 1) * bb, 0:d_conv] = _layernorm_silu(acc, lng_ref[...], lnb_ref[...]).astype(BF16)

    for gi, w in enumerate(POOL_WINDOWS):
        lanes = slice(gi * LANES, (gi + 1) * LANES)
        count = float(min(SAMPLE_POS0 + 1, w))
        for t in range(tt):
            cur = pool_in(n_pool + t, lanes)
            win_sum = cur
            for j in range(1, w):
                win_sum = win_sum + pool_in(n_pool + t - j, lanes)
            pooled = win_sum / count - cur
            mixed = _dot(pooled.astype(BF16), wpool_ref[gi])
            mix_scr[t * bb:(t + 1) * bb, d_conv + gi * LANES:d_conv + (gi + 1) * LANES] = (
                mixed * pscale_ref[:, lanes]).astype(BF16)

    o_ref[...] = x + gate * _dot(mix_scr[...], wout_ref[...]).reshape(tt, bb, d)


def _mixer_sample(x, mod4, gain, w_in, w_dw, b_dw, ln_g, ln_b, w_pool, pool_scale, w_out,
                  state_conv, state_pool, *, batch_block):
    t, b, d = x.shape
    d_conv = w_dw.shape[1]
    d_pool = pool_scale.shape[1]
    n_groups, group = w_pool.shape[0], w_pool.shape[1]
    assert group == LANES and d_pool == n_groups * LANES and d_conv == d_pool
    x_spec = pl.BlockSpec((t, batch_block, d), lambda i: (0, i, 0))
    conv_spec = pl.BlockSpec((1, CONV_WIDTH - 1, batch_block, d_conv), lambda i: (0, 0, i, 0))
    pool_spec = pl.BlockSpec((1, POOL_MAX - 1, batch_block, d_pool), lambda i: (0, 0, i, 0))
    return pl.pallas_call(
        _mixer_sample_kernel,
        grid=(b // batch_block,),
        in_specs=[
            x_spec,
            pl.BlockSpec((3, 1, batch_block, d), lambda i: (1, 0, i, 0)),
            _const_spec((1, d)),
            _const_spec((d, 2 * d_conv + d_pool)),
            _const_spec((CONV_WIDTH, d_conv)),
            _const_spec((1, d_conv)),
            _const_spec((1, d_conv)),
            _const_spec((1, d_conv)),
            _const_spec((n_groups, group, group)),
            _const_spec((1, d_pool)),
            _const_spec((d_conv + d_pool, d)),
            conv_spec,
            pool_spec,
        ],
        out_specs=[x_spec, conv_spec, pool_spec],
        out_shape=[
            jax.ShapeDtypeStruct(x.shape, F32),
            jax.ShapeDtypeStruct(state_conv.shape, F32),
            jax.ShapeDtypeStruct(state_pool.shape, F32),
        ],
        scratch_shapes=[pltpu.VMEM((t * batch_block, d_conv + d_pool), BF16)],
        compiler_params=pltpu.CompilerParams(
            dimension_semantics=("arbitrary",), vmem_limit_bytes=VMEM_LIMIT_BYTES),
        name="mixer_sample",
    )(x, mod4, gain, w_in, w_dw, b_dw, ln_g, ln_b, w_pool, pool_scale, w_out, state_conv, state_pool)


def kernel(x_prompt, x_sample, state_conv, state_pool, c_prompt, c_sample, w_ada, b_ada, g_ffn1, w_ffn1_up, w_ffn1_down, g_mix, w_in, w_dw, b_dw, ln_g, ln_b, w_pool, pool_scale, w_out, g_ffn2, w_ffn2_up, w_ffn2_down, g_final):
    depth = w_ada.shape[0]
    assert depth == 1 and state_conv.shape[0] == 1 and state_pool.shape[0] == 1
    bp, tp, d = x_prompt.shape
    bs, ts, _ = x_sample.shape

    mod_p, mod_s, (w1u, w1d) = _adaln(c_prompt, c_sample, w_ada[0], b_ada[0], col_block=d, cast_slabs=8,
                                      casts=(w_ffn1_up[0], w_ffn1_down[0]))
    mod_p = mod_p.reshape(N_MOD, bp, 1, d)
    mod_s = mod_s.reshape(N_MOD, 1, bs, d)

    g_final2 = g_final.reshape(1, d)
    n_groups, group, _ = w_pool.shape[1:]

    tile_p = min(tp, 1024)
    grid_p = (bp, tp // tile_p)
    x_spec_p = pl.BlockSpec((1, tile_p, d), lambda i, j: (i, j, 0))
    mod_spec_p = lambda mb: pl.BlockSpec((3, 1, 1, d), lambda i, j: (mb, i, 0, 0))
    seq_s = min(bs, 64)
    grid_s = (bs // seq_s,)
    x_spec_s = pl.BlockSpec((ts, seq_s, d), lambda i: (0, i, 0))
    mod_spec_s = lambda mb: pl.BlockSpec((3, 1, seq_s, d), lambda i: (mb, 0, i, 0))
    xs = jnp.transpose(x_sample, (1, 0, 2))
    sconv = jnp.transpose(state_conv, (0, 2, 1, 3))
    spool = jnp.transpose(state_pool, (0, 2, 1, 3))

    later = (w_ffn2_up[0], w_ffn2_down[0], w_in[0], w_out[0], w_pool[0].reshape(n_groups * group, group))
    h_p, (w2u, w2d, w_in_b, w_out_b, w_pool_b) = _ffn(
        x_prompt, x_spec_p, mod_p, mod_spec_p(0), grid_p, g_ffn1, w1u, w1d, None, casts=later)
    mixer_w = (g_mix, w_in_b, w_dw[0], b_dw, ln_g, ln_b, w_pool_b.reshape(n_groups, group, group),
               pool_scale, w_out_b)
    h_s, _ = _ffn(xs, x_spec_s, mod_s, mod_spec_s(0), grid_s, g_ffn1, w1u, w1d, None)

    h_p, conv_p, pool_p = _mixer_prompt(h_p, mod_p, *mixer_w, time_block=tile_p, sub_block=min(tile_p, 256))
    h_s, conv_s, pool_s = _mixer_sample(h_s, mod_s, *mixer_w, sconv, spool, batch_block=min(bs, 32))

    y_p, _ = _ffn(h_p, x_spec_p, mod_p, mod_spec_p(2), grid_p, g_ffn2, w2u, w2d, g_final2)
    y_s, _ = _ffn(h_s, x_spec_s, mod_s, mod_spec_s(2), grid_s, g_ffn2, w2u, w2d, g_final2)
    return (y_p, jnp.transpose(y_s, (1, 0, 2)), conv_p, pool_p,
            jnp.transpose(conv_s, (0, 2, 1, 3)), jnp.transpose(pool_s, (0, 2, 1, 3)))
```

```python
import functools

import jax
import jax.numpy as jnp
from jax import lax
from jax.experimental import pallas as pl
from jax.experimental.pallas import tpu as pltpu

EPS = 1e-6
HALF = 0.5
N_MOD = 9
CONV_WIDTH = 31
POOL_WINDOWS = (2, 4, 8, 16)
POOL_MAX = max(POOL_WINDOWS)
SAMPLE_POS0 = 16384

LANES = 128
BF16_SUBLANES = 16
FIRST_CHUNK_SLABS = 4
CONV_HALO = 32
POOL_HALO = 16
VMEM_LIMIT_BYTES = 56 * 1024 * 1024

F32 = jnp.float32
BF16 = jnp.bfloat16


def _dot(a, b):
    return jnp.dot(a, b, preferred_element_type=F32)


def _sigmoid(x):
    return 1.0 / (1.0 + jnp.exp(-x))


def _rms_scale(x):
    return lax.rsqrt(jnp.mean(x * x, axis=-1, keepdims=True) + EPS)


def _const_spec(shape):
    zeros = (0,) * len(shape)
    return pl.BlockSpec(shape, lambda *_: zeros, pipeline_mode=pl.Buffered(1))


def _cast_specs(casts, n_slabs, slab_of_step):
    specs = []
    for w in casts:
        rows, cols = w.shape
        assert rows % (n_slabs * BF16_SUBLANES) == 0, (w.shape, n_slabs)
        specs.append(pl.BlockSpec((rows // n_slabs, cols), lambda *idx: (slab_of_step(*idx), 0)))
    return specs


def _cast_slabs(cast_in, cast_out):
    for src, dst in zip(cast_in, cast_out):
        dst[...] = src[...].astype(BF16)


def _adaln_kernel(cp_ref, cs_ref, w_ref, b_ref, *rest, n_casts):
    cast_in, (op_ref, os_ref), cast_out = rest[:n_casts], rest[n_casts:n_casts + 2], rest[n_casts + 2:]
    w = w_ref[...].astype(BF16)
    b = b_ref[0]

    def mod(c):
        s = c * _sigmoid(c)
        return _dot(s.astype(BF16), w) + b

    op_ref[0] = mod(cp_ref[...])
    os_ref[0] = mod(cs_ref[...])
    _cast_slabs(cast_in, cast_out)


def _adaln(c_prompt, c_sample, w_ada, b_ada, *, col_block, cast_slabs, casts=()):
    bp, d = c_prompt.shape
    bs = c_sample.shape[0]
    per_mod = d // col_block
    n_steps = N_MOD * per_mod
    assert d % col_block == 0 and n_steps >= cast_slabs
    cast_specs = _cast_specs(casts, cast_slabs, lambda j: jnp.minimum(j, cast_slabs - 1))
    outs = pl.pallas_call(
        functools.partial(_adaln_kernel, n_casts=len(casts)),
        grid=(n_steps,),
        in_specs=[
            pl.BlockSpec((bp, d), lambda j: (0, 0)),
            pl.BlockSpec((bs, d), lambda j: (0, 0)),
            pl.BlockSpec((d, col_block), lambda j: (0, j)),
            pl.BlockSpec((1, 1, col_block), lambda j: (j, 0, 0)),
        ] + cast_specs,
        out_specs=[
            pl.BlockSpec((1, bp, col_block), lambda j: (j // per_mod, 0, j % per_mod)),
            pl.BlockSpec((1, bs, col_block), lambda j: (j // per_mod, 0, j % per_mod)),
        ] + cast_specs,
        out_shape=[
            jax.ShapeDtypeStruct((N_MOD, bp, d), F32),
            jax.ShapeDtypeStruct((N_MOD, bs, d), F32),
        ] + [jax.ShapeDtypeStruct(w.shape, BF16) for w in casts],
        compiler_params=pltpu.CompilerParams(
            dimension_semantics=("arbitrary",), vmem_limit_bytes=VMEM_LIMIT_BYTES),
        name="adaln",
    )(c_prompt, c_sample, w_ada, b_ada.reshape(n_steps, 1, col_block), *casts)
    return outs[0], outs[1], outs[2:]


def _ffn_in(x, mod, gain, xb_scr, r0=0):
    shift, scale, _ = mod
    xn = (x * _rms_scale(x)) * gain * (1.0 + scale) + shift
    m = xn.size // xn.shape[-1]
    xb_scr[r0:r0 + m, :] = xn.reshape(m, xn.shape[-1]).astype(BF16)


def _ffn_chunk(lo, ff_chunk, wup_ref, xb_scr, a_scr, rows=slice(None)):
    d_ff = a_scr.shape[1]
    xb = xb_scr[rows, :]
    g = _dot(xb, wup_ref[:, lo:lo + ff_chunk])
    u = _dot(xb, wup_ref[:, d_ff + lo:d_ff + lo + ff_chunk])
    a_scr[rows, lo:lo + ff_chunk] = (g * _sigmoid(g) * u).astype(BF16)


def _ffn_out(x, mod, wdn_ref, a_scr, g_final, rows=slice(None)):
    gate = mod[2]
    out = x + (HALF * gate) * _dot(a_scr[rows, :], wdn_ref[...]).reshape(x.shape)
    if g_final is not None:
        out = (out * _rms_scale(out)) * g_final
    return out


def _ffn_block(x_ref, mod_ref, o_ref, gain, wup_ref, wdn_ref, xb_scr, a_scr, g_final, ff_chunk):
    mod = (mod_ref[0], mod_ref[1], mod_ref[2])
    lead, rows, _ = x_ref.shape
    used = slice(0, lead * rows)
    if lead == 1 and rows % (FIRST_CHUNK_SLABS * BF16_SUBLANES) == 0:
        slab = rows // FIRST_CHUNK_SLABS
        for r0 in range(0, rows, slab):
            _ffn_in(x_ref[:, r0:r0 + slab, :], mod, gain, xb_scr, r0)
            _ffn_chunk(0, ff_chunk, wup_ref, xb_scr, a_scr, slice(r0, r0 + slab))
        first = ff_chunk
    else:
        _ffn_in(x_ref[...], mod, gain, xb_scr)
        first = 0
    for lo in range(first, a_scr.shape[1], ff_chunk):
        _ffn_chunk(lo, ff_chunk, wup_ref, xb_scr, a_scr, used)
    o_ref[...] = _ffn_out(x_ref[...], mod, wdn_ref, a_scr, g_final, used)


def _ffn_kernel(xp_ref, modp_ref, xs_ref, mods_ref, g_ref, wup_ref, wdn_ref, *rest,
                ff_chunk, final_norm, n_casts, prompt_steps):
    rest = list(rest)
    gf_ref = rest.pop(0) if final_norm else None
    cast_in = [rest.pop(0) for _ in range(n_casts)]
    op_ref, os_ref = rest.pop(0), rest.pop(0)
    cast_out = [rest.pop(0) for _ in range(n_casts)]
    xb_scr, a_scr = rest
    step = pl.program_id(0)
    g_final = gf_ref[...] if final_norm else None

    @pl.when(step < prompt_steps)
    def _():
        _ffn_block(xp_ref, modp_ref, op_ref, g_ref[...], wup_ref, wdn_ref, xb_scr, a_scr, g_final, ff_chunk)

    @pl.when(step >= prompt_steps)
    def _():
        _ffn_block(xs_ref, mods_ref, os_ref, g_ref[...], wup_ref, wdn_ref, xb_scr, a_scr, g_final, ff_chunk)

    _cast_slabs(cast_in, cast_out)


def _ffn(xp, xs, mod_p, mod_s, mod_block, gain, w_up, w_down, g_final, *, tile_p, seq_s, ff_chunk=256, casts=()):
    bp, tp, d = xp.shape
    ts, bs, _ = xs.shape
    d_ff = w_down.shape[0]
    tiles = tp // tile_p
    prompt_steps = bp * tiles
    sample_steps = bs // seq_s
    final_norm = g_final is not None

    def p_tile(s):
        s = jnp.minimum(s, prompt_steps - 1)
        return s // tiles, s % tiles

    def s_block(s):
        return jnp.maximum(s - prompt_steps, 0)

    xp_spec = pl.BlockSpec((1, tile_p, d), lambda s: (*p_tile(s), 0))
    xs_spec = pl.BlockSpec((ts, seq_s, d), lambda s: (0, s_block(s), 0), pipeline_mode=pl.Buffered(1))
    in_specs = [
        xp_spec,
        pl.BlockSpec((3, 1, 1, d), lambda s: (mod_block, p_tile(s)[0], 0, 0)),
        xs_spec,
        pl.BlockSpec((3, 1, seq_s, d), lambda s: (mod_block, 0, s_block(s), 0), pipeline_mode=pl.Buffered(1)),
        _const_spec((1, d)), _const_spec((d, 2 * d_ff)), _const_spec((d_ff, d)),
    ]
    args = [xp, mod_p, xs, mod_s, gain, w_up, w_down]
    if final_norm:
        in_specs.append(_const_spec((1, d)))
        args.append(g_final)
    cast_specs = _cast_specs(casts, prompt_steps, lambda s: jnp.minimum(s, prompt_steps - 1))
    m = max(tile_p, ts * seq_s)
    outs = pl.pallas_call(
        functools.partial(_ffn_kernel, ff_chunk=ff_chunk, final_norm=final_norm, n_casts=len(casts),
                          prompt_steps=prompt_steps),
        grid=(prompt_steps + sample_steps,),
        in_specs=in_specs + cast_specs,
        out_specs=[xp_spec, xs_spec] + cast_specs,
        out_shape=[jax.ShapeDtypeStruct(xp.shape, F32), jax.ShapeDtypeStruct(xs.shape, F32)]
        + [jax.ShapeDtypeStruct(w.shape, BF16) for w in casts],
        scratch_shapes=[pltpu.VMEM((m, d), BF16), pltpu.VMEM((m, d_ff), BF16)],
        compiler_params=pltpu.CompilerParams(
            dimension_semantics=("arbitrary",), vmem_limit_bytes=VMEM_LIMIT_BYTES),
        name="ffn_final" if final_norm else "ffn",
    )(*args, *casts)
    return outs[0], outs[1], outs[2:]


def _layernorm_silu(v, ln_g, ln_b):
    mu = jnp.mean(v, axis=-1, keepdims=True)
    c = v - mu
    var = jnp.mean(c * c, axis=-1, keepdims=True)
    y = (c * lax.rsqrt(var + EPS)) * ln_g + ln_b
    return y * _sigmoid(y)


def _mixer_rows_in(x, mod, gain, win_ref, cs, ps, r0):
    sb = x.shape[0]
    n_slabs = cs.shape[0]
    d_conv = n_slabs * LANES
    shift, scale, _ = mod
    hm = (x * _rms_scale(x)) * gain * (1.0 + scale) + shift
    proj = _dot(hm.astype(BF16), win_ref[...])
    u_conv = proj[:, :d_conv] * _sigmoid(proj[:, d_conv:2 * d_conv])
    for s in range(n_slabs):
        cs[s, CONV_HALO + r0:CONV_HALO + r0 + sb, :] = u_conv[:, s * LANES:(s + 1) * LANES]
        ps[s, POOL_HALO + r0:POOL_HALO + r0 + sb, :] = (
            proj[:, 2 * d_conv + s * LANES:2 * d_conv + (s + 1) * LANES])


def _mixer_rows_out(x, mod, pos0, r0, row_block, wdw_ref, bdw_ref, lng_ref, lnb_ref,
                    wpool_ref, pscale_ref, wout_ref, cs, ps, v_scr, mix_scr):
    sb = x.shape[0]
    n_slabs = cs.shape[0]
    d_conv = n_slabs * LANES
    assert len(POOL_WINDOWS) == ps.shape[0] and wpool_ref.shape[1] == LANES
    gate = mod[2]
    rows = slice(r0, r0 + sb)

    first = CONV_HALO - (CONV_WIDTH - 1)
    for s in range(n_slabs):
        lanes = slice(s * LANES, (s + 1) * LANES)
        for r in range(r0, r0 + sb, row_block):
            acc = jnp.broadcast_to(bdw_ref[:, lanes], (row_block, LANES))
            for k in range(CONV_WIDTH):
                acc = acc + wdw_ref[k:k + 1, lanes] * cs[s, first + k + r:first + k + r + row_block, :]
            v_scr[r:r + row_block, lanes] = acc
    mix_scr[rows, 0:d_conv] = _layernorm_silu(v_scr[rows, :], lng_ref[...], lnb_ref[...]).astype(BF16)

    pos = pos0 + r0 + lax.broadcasted_iota(jnp.int32, (sb, 1), 0)
    for gi, w in enumerate(POOL_WINDOWS):
        lanes = slice(gi * LANES, (gi + 1) * LANES)
        cur = ps[gi, POOL_HALO + r0:POOL_HALO + r0 + sb, :]
        win_sum = cur
        for j in range(1, w):
            win_sum = win_sum + ps[gi, POOL_HALO + r0 - j:POOL_HALO + r0 - j + sb, :]
        count = jnp.minimum(pos + 1, w).astype(F32)
        pooled = win_sum / count - cur
        mixed = _dot(pooled.astype(BF16), wpool_ref[gi])
        mix_scr[rows, d_conv + gi * LANES:d_conv + (gi + 1) * LANES] = (
            mixed * pscale_ref[:, lanes]).astype(BF16)

    return x + gate * _dot(mix_scr[rows, :], wout_ref[...])


def _mixer_prompt_kernel(x_ref, mod_ref, g_ref, win_ref, wdw_ref, bdw_ref, lng_ref, lnb_ref,
                         wpool_ref, pscale_ref, wout_ref,
                         o_ref, oconv_ref, opool_ref,
                         cs, ps, v_scr, mix_scr, *, sub_block, row_block):
    t_idx = pl.program_id(1)
    n_t = pl.num_programs(1)
    tm = x_ref.shape[1]
    n_slabs = cs.shape[0]

    @pl.when(t_idx == 0)
    def _():
        cs[:, 0:CONV_HALO, :] = jnp.zeros((n_slabs, CONV_HALO, LANES), F32)
        ps[:, 0:POOL_HALO, :] = jnp.zeros((ps.shape[0], POOL_HALO, LANES), F32)

    mod = (mod_ref[0, 0], mod_ref[1, 0], mod_ref[2, 0])
    for r0 in range(0, tm, sub_block):
        x = x_ref[0, r0:r0 + sub_block, :]
        _mixer_rows_in(x, mod, g_ref[...], win_ref, cs, ps, r0)
        o_ref[0, r0:r0 + sub_block, :] = _mixer_rows_out(
            x, mod, t_idx * tm, r0, row_block, wdw_ref, bdw_ref,
            lng_ref, lnb_ref, wpool_ref, pscale_ref, wout_ref, cs, ps, v_scr, mix_scr)

    @pl.when(t_idx == n_t - 1)
    def _():
        for s in range(n_slabs):
            lanes = slice(s * LANES, (s + 1) * LANES)
            oconv_ref[0, 0, :, lanes] = cs[s, CONV_HALO + tm - (CONV_WIDTH - 1):CONV_HALO + tm, :]
            opool_ref[0, 0, :, lanes] = ps[s, POOL_HALO + tm - (POOL_MAX - 1):POOL_HALO + tm, :]

    cs[:, 0:CONV_HALO, :] = cs[:, tm:tm + CONV_HALO, :]
    ps[:, 0:POOL_HALO, :] = ps[:, tm:tm + POOL_HALO, :]


def _mixer_prompt(x, mod4, gain, w_in, w_dw, b_dw, ln_g, ln_b, w_pool, pool_scale, w_out,
                  *, time_block, sub_block, row_block=64):
    b, t, d = x.shape
    d_conv = w_dw.shape[1]
    d_pool = pool_scale.shape[1]
    n_groups, group = w_pool.shape[0], w_pool.shape[1]
    assert group == LANES and d_conv % LANES == 0 and d_pool == n_groups * LANES
    assert time_block % sub_block == 0 and sub_block % row_block == 0
    x_spec = pl.BlockSpec((1, time_block, d), lambda i, j: (i, j, 0))
    return pl.pallas_call(
        functools.partial(_mixer_prompt_kernel, sub_block=sub_block, row_block=row_block),
        grid=(b, t // time_block),
        in_specs=[
            x_spec,
            pl.BlockSpec((3, 1, 1, d), lambda i, j: (1, i, 0, 0)),
            _const_spec((1, d)),
            _const_spec((d, 2 * d_conv + d_pool)),
            _const_spec((CONV_WIDTH, d_conv)),
            _const_spec((1, d_conv)),
            _const_spec((1, d_conv)),
            _const_spec((1, d_conv)),
            _const_spec((n_groups, group, group)),
            _const_spec((1, d_pool)),
            _const_spec((d_conv + d_pool, d)),
        ],
        out_specs=[
            x_spec,
            pl.BlockSpec((1, 1, CONV_WIDTH - 1, d_conv), lambda i, j: (0, i, 0, 0)),
            pl.BlockSpec((1, 1, POOL_MAX - 1, d_pool), lambda i, j: (0, i, 0, 0)),
        ],
        out_shape=[
            jax.ShapeDtypeStruct(x.shape, F32),
            jax.ShapeDtypeStruct((1, b, CONV_WIDTH - 1, d_conv), F32),
            jax.ShapeDtypeStruct((1, b, POOL_MAX - 1, d_pool), F32),
        ],
        scratch_shapes=[
            pltpu.VMEM((d_conv // LANES, CONV_HALO + time_block, LANES), F32),
            pltpu.VMEM((n_groups, POOL_HALO + time_block, LANES), F32),
            pltpu.VMEM((time_block, d_conv), F32),
            pltpu.VMEM((time_block, d_conv + d_pool), BF16),
        ],
        compiler_params=pltpu.CompilerParams(
            dimension_semantics=("arbitrary", "arbitrary"), vmem_limit_bytes=VMEM_LIMIT_BYTES),
        name="mixer_prompt",
    )(x, mod4, gain, w_in, w_dw, b_dw, ln_g, ln_b, w_pool, pool_scale, w_out)


def _mixer_sample_kernel(x_ref, mod_ref, g_ref, win_ref, wdw_ref, bdw_ref, lng_ref, lnb_ref,
                         wpool_ref, pscale_ref, wout_ref, sconv_ref, spool_ref,
                         o_ref, oconv_ref, opool_ref, mix_scr):
    tt, bb, d = x_ref.shape
    m = tt * bb
    n_conv, d_conv = sconv_ref.shape[1], sconv_ref.shape[3]
    n_pool = spool_ref.shape[1]

    x = x_ref[...]
    shift, scale, gate = mod_ref[0], mod_ref[1], mod_ref[2]
    hm = (x * _rms_scale(x)) * g_ref[...] * (1.0 + scale) + shift
    proj = _dot(hm.reshape(m, d).astype(BF16), win_ref[...])
    u_conv = proj[:, :d_conv] * _sigmoid(proj[:, d_conv:2 * d_conv])
    u_pool = proj[:, 2 * d_conv:]

    def conv_in(s):
        return sconv_ref[0, s] if s < n_conv else u_conv[(s - n_conv) * bb:(s - n_conv + 1) * bb, :]

    def pool_in(s, lanes=slice(None)):
        return spool_ref[0, s, :, lanes] if s < n_pool else u_pool[(s - n_pool) * bb:(s - n_pool + 1) * bb, lanes]

    for s in range(n_conv):
        oconv_ref[0, s] = conv_in(s + tt)
    for s in range(n_pool):
        opool_ref[0, s] = pool_in(s + tt)

    for t in range(tt):
        acc = jnp.broadcast_to(bdw_ref[...], (bb, d_conv))
        for k in range(CONV_WIDTH):
            acc = acc + wdw_ref[k:k + 1, :] * conv_in(t + k)
        mix_scr[t * bb:(t + 1) * bb, 0:d_conv] = _layernorm_silu(acc, lng_ref[...], lnb_ref[...]).astype(BF16)

    for gi, w in enumerate(POOL_WINDOWS):
        lanes = slice(gi * LANES, (gi + 1) * LANES)
        count = float(min(SAMPLE_POS0 + 1, w))
        for t in range(tt):
            cur = pool_in(n_pool + t, lanes)
            win_sum = cur
            for j in range(1, w):
                win_sum = win_sum + pool_in(n_pool + t - j, lanes)
            pooled = win_sum / count - cur
            mixed = _dot(pooled.astype(BF16), wpool_ref[gi])
            mix_scr[t * bb:(t + 1) * bb, d_conv + gi * LANES:d_conv + (gi + 1) * LANES] = (
                mixed * pscale_ref[:, lanes]).astype(BF16)

    o_ref[...] = x + gate * _dot(mix_scr[...], wout_ref[...]).reshape(tt, bb, d)


def _mixer_sample(x, mod4, gain, w_in, w_dw, b_dw, ln_g, ln_b, w_pool, pool_scale, w_out,
                  state_conv, state_pool, *, batch_block):
    t, b, d = x.shape
    d_conv = w_dw.shape[1]
    d_pool = pool_scale.shape[1]
    n_groups, group = w_pool.shape[0], w_pool.shape[1]
    assert group == LANES and d_pool == n_groups * LANES and d_conv == d_pool
    x_spec = pl.BlockSpec((t, batch_block, d), lambda i: (0, i, 0))
    conv_spec = pl.BlockSpec((1, CONV_WIDTH - 1, batch_block, d_conv), lambda i: (0, 0, i, 0))
    pool_spec = pl.BlockSpec((1, POOL_MAX - 1, batch_block, d_pool), lambda i: (0, 0, i, 0))
    return pl.pallas_call(
        _mixer_sample_kernel,
        grid=(b // batch_block,),
        in_specs=[
            x_spec,
            pl.BlockSpec((3, 1, batch_block, d), lambda i: (1, 0, i, 0)),
            _const_spec((1, d)),
            _const_spec((d, 2 * d_conv + d_pool)),
            _const_spec((CONV_WIDTH, d_conv)),
            _const_spec((1, d_conv)),
            _const_spec((1, d_conv)),
            _const_spec((1, d_conv)),
            _const_spec((n_groups, group, group)),
            _const_spec((1, d_pool)),
            _const_spec((d_conv + d_pool, d)),
            conv_spec,
            pool_spec,
        ],
        out_specs=[x_spec, conv_spec, pool_spec],
        out_shape=[
            jax.ShapeDtypeStruct(x.shape, F32),
            jax.ShapeDtypeStruct(state_conv.shape, F32),
            jax.ShapeDtypeStruct(state_pool.shape, F32),
        ],
        scratch_shapes=[pltpu.VMEM((t * batch_block, d_conv + d_pool), BF16)],
        compiler_params=pltpu.CompilerParams(
            dimension_semantics=("arbitrary",), vmem_limit_bytes=VMEM_LIMIT_BYTES),
        name="mixer_sample",
    )(x, mod4, gain, w_in, w_dw, b_dw, ln_g, ln_b, w_pool, pool_scale, w_out, state_conv, state_pool)


def kernel(x_prompt, x_sample, state_conv, state_pool, c_prompt, c_sample, w_ada, b_ada, g_ffn1, w_ffn1_up, w_ffn1_down, g_mix, w_in, w_dw, b_dw, ln_g, ln_b, w_pool, pool_scale, w_out, g_ffn2, w_ffn2_up, w_ffn2_down, g_final):
    depth = w_ada.shape[0]
    assert depth == 1 and state_conv.shape[0] == 1 and state_pool.shape[0] == 1
    bp, tp, d = x_prompt.shape
    bs, ts, _ = x_sample.shape

    mod_p, mod_s, (w1u, w1d) = _adaln(c_prompt, c_sample, w_ada[0], b_ada[0], col_block=d, cast_slabs=8,
                                      casts=(w_ffn1_up[0], w_ffn1_down[0]))
    mod_p = mod_p.reshape(N_MOD, bp, 1, d)
    mod_s = mod_s.reshape(N_MOD, 1, bs, d)

    g_final2 = g_final.reshape(1, d)
    n_groups, group, _ = w_pool.shape[1:]

    tiles = dict(tile_p=min(tp, 1024), seq_s=min(bs, 64))
    xs = jnp.transpose(x_sample, (1, 0, 2))
    sconv = jnp.transpose(state_conv, (0, 2, 1, 3))
    spool = jnp.transpose(state_pool, (0, 2, 1, 3))

    later = (w_ffn2_up[0], w_ffn2_down[0], w_in[0], w_out[0], w_pool[0].reshape(n_groups * group, group))
    h_p, h_s, (w2u, w2d, w_in_b, w_out_b, w_pool_b) = _ffn(
        x_prompt, xs, mod_p, mod_s, 0, g_ffn1, w1u, w1d, None, casts=later, **tiles)
    mixer_w = (g_mix, w_in_b, w_dw[0], b_dw, ln_g, ln_b, w_pool_b.reshape(n_groups, group, group),
               pool_scale, w_out_b)

    h_p, conv_p, pool_p = _mixer_prompt(h_p, mod_p, *mixer_w, time_block=min(tp, 1024), sub_block=min(tp, 256))
    h_s, conv_s, pool_s = _mixer_sample(h_s, mod_s, *mixer_w, sconv, spool, batch_block=min(bs, 32))

    y_p, y_s, _ = _ffn(h_p, h_s, mod_p, mod_s, 2, g_ffn2, w2u, w2d, g_final2, **tiles)
    return (y_p, jnp.transpose(y_s, (1, 0, 2)), conv_p, pool_p,
            jnp.transpose(conv_s, (0, 2, 1, 3)), jnp.transpose(pool_s, (0, 2, 1, 3)))
```

```python
import functools

import jax
import jax.numpy as jnp
from jax import lax
from jax.experimental import pallas as pl
from jax.experimental.pallas import tpu as pltpu

EPS = 1e-6
HALF = 0.5
N_MOD = 9
CONV_WIDTH = 31
POOL_WINDOWS = (2, 4, 8, 16)
POOL_MAX = max(POOL_WINDOWS)
SAMPLE_POS0 = 16384

LANES = 128
BF16_SUBLANES = 16
FIRST_CHUNK_SLABS = 4
CONV_HALO = 32
POOL_HALO = 16
VMEM_LIMIT_BYTES = 56 * 1024 * 1024

F32 = jnp.float32
BF16 = jnp.bfloat16


def _dot(a, b):
    return jnp.dot(a, b, preferred_element_type=F32)


def _sigmoid(x):
    return 1.0 / (1.0 + jnp.exp(-x))


def _rms_scale(x):
    return lax.rsqrt(jnp.mean(x * x, axis=-1, keepdims=True) + EPS)


def _const_spec(shape):
    zeros = (0,) * len(shape)
    return pl.BlockSpec(shape, lambda *_: zeros, pipeline_mode=pl.Buffered(1))


def _cast_specs(casts, n_slabs, slab_of_step):
    specs = []
    for w in casts:
        rows, cols = w.shape
        assert rows % (n_slabs * BF16_SUBLANES) == 0, (w.shape, n_slabs)
        specs.append(pl.BlockSpec((rows // n_slabs, cols), lambda *idx: (slab_of_step(*idx), 0)))
    return specs


def _cast_slabs(cast_in, cast_out):
    for src, dst in zip(cast_in, cast_out):
        dst[...] = src[...].astype(BF16)


def _adaln_kernel(cp_ref, cs_ref, w_ref, b_ref, *rest, n_casts):
    cast_in, (op_ref, os_ref), cast_out = rest[:n_casts], rest[n_casts:n_casts + 2], rest[n_casts + 2:]
    w = w_ref[...].astype(BF16)
    b = b_ref[0]

    def mod(c):
        s = c * _sigmoid(c)
        return _dot(s.astype(BF16), w) + b

    op_ref[0] = mod(cp_ref[...])
    os_ref[0] = mod(cs_ref[...])
    _cast_slabs(cast_in, cast_out)


def _adaln(c_prompt, c_sample, w_ada, b_ada, *, col_block, cast_slabs, casts=()):
    bp, d = c_prompt.shape
    bs = c_sample.shape[0]
    per_mod = d // col_block
    n_steps = N_MOD * per_mod
    assert d % col_block == 0 and n_steps >= cast_slabs
    cast_specs = _cast_specs(casts, cast_slabs, lambda j: jnp.minimum(j, cast_slabs - 1))
    outs = pl.pallas_call(
        functools.partial(_adaln_kernel, n_casts=len(casts)),
        grid=(n_steps,),
        in_specs=[
            pl.BlockSpec((bp, d), lambda j: (0, 0)),
            pl.BlockSpec((bs, d), lambda j: (0, 0)),
            pl.BlockSpec((d, col_block), lambda j: (0, j)),
            pl.BlockSpec((1, 1, col_block), lambda j: (j, 0, 0)),
        ] + cast_specs,
        out_specs=[
            pl.BlockSpec((1, bp, col_block), lambda j: (j // per_mod, 0, j % per_mod)),
            pl.BlockSpec((1, bs, col_block), lambda j: (j // per_mod, 0, j % per_mod)),
        ] + cast_specs,
        out_shape=[
            jax.ShapeDtypeStruct((N_MOD, bp, d), F32),
            jax.ShapeDtypeStruct((N_MOD, bs, d), F32),
        ] + [jax.ShapeDtypeStruct(w.shape, BF16) for w in casts],
        compiler_params=pltpu.CompilerParams(
            dimension_semantics=("arbitrary",), vmem_limit_bytes=VMEM_LIMIT_BYTES),
        name="adaln",
    )(c_prompt, c_sample, w_ada, b_ada.reshape(n_steps, 1, col_block), *casts)
    return outs[0], outs[1], outs[2:]


def _ffn_in(x, mod, gain, xb_scr, r0=0):
    shift, scale, _ = mod
    xn = (x * _rms_scale(x)) * gain * (1.0 + scale) + shift
    m = xn.size // xn.shape[-1]
    xb_scr[r0:r0 + m, :] = xn.reshape(m, xn.shape[-1]).astype(BF16)


def _ffn_chunk(lo, ff_chunk, wup_ref, xb_scr, a_scr, rows=slice(None)):
    d_ff = a_scr.shape[1]
    xb = xb_scr[rows, :]
    g = _dot(xb, wup_ref[:, lo:lo + ff_chunk])
    u = _dot(xb, wup_ref[:, d_ff + lo:d_ff + lo + ff_chunk])
    a_scr[rows, lo:lo + ff_chunk] = (g * _sigmoid(g) * u).astype(BF16)


def _ffn_out(x, mod, wdn_ref, a_scr, g_final):
    gate = mod[2]
    out = x + (HALF * gate) * _dot(a_scr[...], wdn_ref[...]).reshape(x.shape)
    if g_final is not None:
        out = (out * _rms_scale(out)) * g_final
    return out


def _ffn_kernel(x_ref, mod_ref, g_ref, wup_ref, wdn_ref, *rest, ff_chunk, final_norm, n_casts):
    rest = list(rest)
    gf_ref = rest.pop(0) if final_norm else None
    cast_in = [rest.pop(0) for _ in range(n_casts)]
    o_ref = rest.pop(0)
    cast_out = [rest.pop(0) for _ in range(n_casts)]
    xb_scr, a_scr = rest
    mod = (mod_ref[0], mod_ref[1], mod_ref[2])
    lead, rows, _ = x_ref.shape
    if lead == 1 and rows % (FIRST_CHUNK_SLABS * BF16_SUBLANES) == 0:
        slab = rows // FIRST_CHUNK_SLABS
        for r0 in range(0, rows, slab):
            _ffn_in(x_ref[:, r0:r0 + slab, :], mod, g_ref[...], xb_scr, r0)
            _ffn_chunk(0, ff_chunk, wup_ref, xb_scr, a_scr, slice(r0, r0 + slab))
        first = ff_chunk
    else:
        _ffn_in(x_ref[...], mod, g_ref[...], xb_scr)
        first = 0
    for lo in range(first, a_scr.shape[1], ff_chunk):
        _ffn_chunk(lo, ff_chunk, wup_ref, xb_scr, a_scr)
    o_ref[...] = _ffn_out(x_ref[...], mod, wdn_ref, a_scr, gf_ref[...] if final_norm else None)
    _cast_slabs(cast_in, cast_out)


def _ffn(x, x_spec, mod4, mod_spec, grid, gain, w_up, w_down, g_final, *, ff_chunk=256, casts=()):
    d = x.shape[-1]
    d_ff = w_down.shape[0]
    m = x_spec.block_shape[0] * x_spec.block_shape[1]
    final_norm = g_final is not None
    in_specs = [x_spec, mod_spec, _const_spec((1, d)), _const_spec((d, 2 * d_ff)), _const_spec((d_ff, d))]
    args = [x, mod4, gain, w_up, w_down]
    if final_norm:
        in_specs.append(_const_spec((1, d)))
        args.append(g_final)
    n_steps = 1
    for g in grid:
        n_steps *= g

    def flat_step(*idx):
        s = idx[0]
        for i, g in zip(idx[1:], grid[1:]):
            s = s * g + i
        return s

    cast_specs = _cast_specs(casts, n_steps, flat_step)
    outs = pl.pallas_call(
        functools.partial(_ffn_kernel, ff_chunk=ff_chunk, final_norm=final_norm, n_casts=len(casts)),
        grid=grid,
        in_specs=in_specs + cast_specs,
        out_specs=[x_spec] + cast_specs,
        out_shape=[jax.ShapeDtypeStruct(x.shape, F32)] + [jax.ShapeDtypeStruct(w.shape, BF16) for w in casts],
        scratch_shapes=[pltpu.VMEM((m, d), BF16), pltpu.VMEM((m, d_ff), BF16)],
        compiler_params=pltpu.CompilerParams(
            dimension_semantics=("arbitrary",) * len(grid), vmem_limit_bytes=VMEM_LIMIT_BYTES),
        name="ffn_final" if final_norm else "ffn",
    )(*args, *casts)
    return outs[0], outs[1:]


def _layernorm_silu(v, ln_g, ln_b):
    mu = jnp.mean(v, axis=-1, keepdims=True)
    c = v - mu
    var = jnp.mean(c * c, axis=-1, keepdims=True)
    y = (c * lax.rsqrt(var + EPS)) * ln_g + ln_b
    return y * _sigmoid(y)


def _mixer_rows_in(x, mod, gain, win_ref, cs, ps, r0):
    sb = x.shape[0]
    n_slabs = cs.shape[0]
    d_conv = n_slabs * LANES
    shift, scale, _ = mod
    hm = (x * _rms_scale(x)) * gain * (1.0 + scale) + shift
    proj = _dot(hm.astype(BF16), win_ref[...])
    u_conv = proj[:, :d_conv] * _sigmoid(proj[:, d_conv:2 * d_conv])
    for s in range(n_slabs):
        cs[s, CONV_HALO + r0:CONV_HALO + r0 + sb, :] = u_conv[:, s * LANES:(s + 1) * LANES]
        ps[s, POOL_HALO + r0:POOL_HALO + r0 + sb, :] = (
            proj[:, 2 * d_conv + s * LANES:2 * d_conv + (s + 1) * LANES])


def _mixer_rows_out(x, mod, pos0, r0, row_block, wdw_ref, bdw_ref, lng_ref, lnb_ref,
                    wpool_ref, pscale_ref, wout_ref, cs, ps, v_scr, mix_scr):
    sb = x.shape[0]
    n_slabs = cs.shape[0]
    d_conv = n_slabs * LANES
    assert len(POOL_WINDOWS) == ps.shape[0] and wpool_ref.shape[1] == LANES
    gate = mod[2]
    rows = slice(r0, r0 + sb)

    first = CONV_HALO - (CONV_WIDTH - 1)
    for s in range(n_slabs):
        lanes = slice(s * LANES, (s + 1) * LANES)
        for r in range(r0, r0 + sb, row_block):
            acc = jnp.broadcast_to(bdw_ref[:, lanes], (row_block, LANES))
            for k in range(CONV_WIDTH):
                acc = acc + wdw_ref[k:k + 1, lanes] * cs[s, first + k + r:first + k + r + row_block, :]
            v_scr[r:r + row_block, lanes] = acc
    mix_scr[rows, 0:d_conv] = _layernorm_silu(v_scr[rows, :], lng_ref[...], lnb_ref[...]).astype(BF16)

    pos = pos0 + r0 + lax.broadcasted_iota(jnp.int32, (sb, 1), 0)
    for gi, w in enumerate(POOL_WINDOWS):
        lanes = slice(gi * LANES, (gi + 1) * LANES)
        cur = ps[gi, POOL_HALO + r0:POOL_HALO + r0 + sb, :]
        win_sum = cur
        for j in range(1, w):
            win_sum = win_sum + ps[gi, POOL_HALO + r0 - j:POOL_HALO + r0 - j + sb, :]
        count = jnp.minimum(pos + 1, w).astype(F32)
        pooled = win_sum / count - cur
        mixed = _dot(pooled.astype(BF16), wpool_ref[gi])
        mix_scr[rows, d_conv + gi * LANES:d_conv + (gi + 1) * LANES] = (
            mixed * pscale_ref[:, lanes]).astype(BF16)

    return x + gate * _dot(mix_scr[rows, :], wout_ref[...])


def _mixer_prompt_kernel(x_ref, mod_ref, g_ref, win_ref, wdw_ref, bdw_ref, lng_ref, lnb_ref,
                         wpool_ref, pscale_ref, wout_ref,
                         o_ref, oconv_ref, opool_ref,
                         cs, ps, v_scr, mix_scr, *, sub_block, row_block):
    t_idx = pl.program_id(1)
    n_t = pl.num_programs(1)
    tm = x_ref.shape[1]
    n_slabs = cs.shape[0]

    @pl.when(t_idx == 0)
    def _():
        cs[:, 0:CONV_HALO, :] = jnp.zeros((n_slabs, CONV_HALO, LANES), F32)
        ps[:, 0:POOL_HALO, :] = jnp.zeros((ps.shape[0], POOL_HALO, LANES), F32)

    mod = (mod_ref[0, 0], mod_ref[1, 0], mod_ref[2, 0])
    for r0 in range(0, tm, sub_block):
        x = x_ref[0, r0:r0 + sub_block, :]
        _mixer_rows_in(x, mod, g_ref[...], win_ref, cs, ps, r0)
        o_ref[0, r0:r0 + sub_block, :] = _mixer_rows_out(
            x, mod, t_idx * tm, r0, row_block, wdw_ref, bdw_ref,
            lng_ref, lnb_ref, wpool_ref, pscale_ref, wout_ref, cs, ps, v_scr, mix_scr)

    @pl.when(t_idx == n_t - 1)
    def _():
        for s in range(n_slabs):
            lanes = slice(s * LANES, (s + 1) * LANES)
            oconv_ref[0, 0, :, lanes] = cs[s, CONV_HALO + tm - (CONV_WIDTH - 1):CONV_HALO + tm, :]
            opool_ref[0, 0, :, lanes] = ps[s, POOL_HALO + tm - (POOL_MAX - 1):POOL_HALO + tm, :]

    cs[:, 0:CONV_HALO, :] = cs[:, tm:tm + CONV_HALO, :]
    ps[:, 0:POOL_HALO, :] = ps[:, tm:tm + POOL_HALO, :]


def _mixer_prompt(x, mod4, gain, w_in, w_dw, b_dw, ln_g, ln_b, w_pool, pool_scale, w_out,
                  *, time_block, sub_block, row_block=16):
    b, t, d = x.shape
    d_conv = w_dw.shape[1]
    d_pool = pool_scale.shape[1]
    n_groups, group = w_pool.shape[0], w_pool.shape[1]
    assert group == LANES and d_conv % LANES == 0 and d_pool == n_groups * LANES
    assert time_block % sub_block == 0 and sub_block % row_block == 0
    x_spec = pl.BlockSpec((1, time_block, d), lambda i, j: (i, j, 0))
    return pl.pallas_call(
        functools.partial(_mixer_prompt_kernel, sub_block=sub_block, row_block=row_block),
        grid=(b, t // time_block),
        in_specs=[
            x_spec,
            pl.BlockSpec((3, 1, 1, d), lambda i, j: (1, i, 0, 0)),
            _const_spec((1, d)),
            _const_spec((d, 2 * d_conv + d_pool)),
            _const_spec((CONV_WIDTH, d_conv)),
            _const_spec((1, d_conv)),
            _const_spec((1, d_conv)),
            _const_spec((1, d_conv)),
            _const_spec((n_groups, group, group)),
            _const_spec((1, d_pool)),
            _const_spec((d_conv + d_pool, d)),
        ],
        out_specs=[
            x_spec,
            pl.BlockSpec((1, 1, CONV_WIDTH - 1, d_conv), lambda i, j: (0, i, 0, 0)),
            pl.BlockSpec((1, 1, POOL_MAX - 1, d_pool), lambda i, j: (0, i, 0, 0)),
        ],
        out_shape=[
            jax.ShapeDtypeStruct(x.shape, F32),
            jax.ShapeDtypeStruct((1, b, CONV_WIDTH - 1, d_conv), F32),
            jax.ShapeDtypeStruct((1, b, POOL_MAX - 1, d_pool), F32),
        ],
        scratch_shapes=[
            pltpu.VMEM((d_conv // LANES, CONV_HALO + time_block, LANES), F32),
            pltpu.VMEM((n_groups, POOL_HALO + time_block, LANES), F32),
            pltpu.VMEM((time_block, d_conv), F32),
            pltpu.VMEM((time_block, d_conv + d_pool), BF16),
        ],
        compiler_params=pltpu.CompilerParams(
            dimension_semantics=("arbitrary", "arbitrary"), vmem_limit_bytes=VMEM_LIMIT_BYTES),
        name="mixer_prompt",
    )(x, mod4, gain, w_in, w_dw, b_dw, ln_g, ln_b, w_pool, pool_scale, w_out)


def _mixer_sample_kernel(x_ref, mod_ref, g_ref, win_ref, wdw_ref, bdw_ref, lng_ref, lnb_ref,
                         wpool_ref, pscale_ref, wout_ref, sconv_ref, spool_ref,
                         o_ref, oconv_ref, opool_ref, mix_scr):
    tt, bb, d = x_ref.shape
    m = tt * bb
    n_conv, d_conv = sconv_ref.shape[1], sconv_ref.shape[3]
    n_pool = spool_ref.shape[1]

    x = x_ref[...]
    shift, scale, gate = mod_ref[0], mod_ref[1], mod_ref[2]
    hm = (x * _rms_scale(x)) * g_ref[...] * (1.0 + scale) + shift
    proj = _dot(hm.reshape(m, d).astype(BF16), win_ref[...])
    u_conv = proj[:, :d_conv] * _sigmoid(proj[:, d_conv:2 * d_conv])
    u_pool = proj[:, 2 * d_conv:]

    def conv_in(s):
        return sconv_ref[0, s] if s < n_conv else u_conv[(s - n_conv) * bb:(s - n_conv + 1) * bb, :]

    def pool_in(s, lanes=slice(None)):
        return spool_ref[0, s, :, lanes] if s < n_pool else u_pool[(s - n_pool) * bb:(s - n_pool + 1) * bb, lanes]

    for s in range(n_conv):
        oconv_ref[0, s] = conv_in(s + tt)
    for s in range(n_pool):
        opool_ref[0, s] = pool_in(s + tt)

    for t in range(tt):
        acc = jnp.broadcast_to(bdw_ref[...], (bb, d_conv))
        for k in range(CONV_WIDTH):
            acc = acc + wdw_ref[k:k + 1, :] * conv_in(t + k)
        mix_scr[t * bb:(t + 1) * bb, 0:d_conv] = _layernorm_silu(acc, lng_ref[...], lnb_ref[...]).astype(BF16)

    for gi, w in enumerate(POOL_WINDOWS):
        lanes = slice(gi * LANES, (gi + 1) * LANES)
        count = float(min(SAMPLE_POS0 + 1, w))
        for t in range(tt):
            cur = pool_in(n_pool + t, lanes)
            win_sum = cur
            for j in range(1, w):
                win_sum = win_sum + pool_in(n_pool + t - j, lanes)
            pooled = win_sum / count - cur
            mixed = _dot(pooled.astype(BF16), wpool_ref[gi])
            mix_scr[t * bb:(t + 1) * bb, d_conv + gi * LANES:d_conv + (gi + 1) * LANES] = (
                mixed * pscale_ref[:, lanes]).astype(BF16)

    o_ref[...] = x + gate * _dot(mix_scr[...], wout_ref[...]).reshape(tt, bb, d)


def _mixer_sample(x, mod4, gain, w_in, w_dw, b_dw, ln_g, ln_b, w_pool, pool_scale, w_out,
                  state_conv, state_pool, *, batch_block):
    t, b, d = x.shape
    d_conv = w_dw.shape[1]
    d_pool = pool_scale.shape[1]
    n_groups, group = w_pool.shape[0], w_pool.shape[1]
    assert group == LANES and d_pool == n_groups * LANES and d_conv == d_pool
    x_spec = pl.BlockSpec((t, batch_block, d), lambda i: (0, i, 0))
    conv_spec = pl.BlockSpec((1, CONV_WIDTH - 1, batch_block, d_conv), lambda i: (0, 0, i, 0))
    pool_spec = pl.BlockSpec((1, POOL_MAX - 1, batch_block, d_pool), lambda i: (0, 0, i, 0))
    return pl.pallas_call(
        _mixer_sample_kernel,
        grid=(b // batch_block,),
        in_specs=[
            x_spec,
            pl.BlockSpec((3, 1, batch_block, d), lambda i: (1, 0, i, 0)),
            _const_spec((1, d)),
            _const_spec((d, 2 * d_conv + d_pool)),
            _const_spec((CONV_WIDTH, d_conv)),
            _const_spec((1, d_conv)),
            _const_spec((1, d_conv)),
            _const_spec((1, d_conv)),
            _const_spec((n_groups, group, group)),
            _const_spec((1, d_pool)),
            _const_spec((d_conv + d_pool, d)),
            conv_spec,
            pool_spec,
        ],
        out_specs=[x_spec, conv_spec, pool_spec],
        out_shape=[
            jax.ShapeDtypeStruct(x.shape, F32),
            jax.ShapeDtypeStruct(state_conv.shape, F32),
            jax.ShapeDtypeStruct(state_pool.shape, F32),
        ],
        scratch_shapes=[pltpu.VMEM((t * batch_block, d_conv + d_pool), BF16)],
        compiler_params=pltpu.CompilerParams(
            dimension_semantics=("arbitrary",), vmem_limit_bytes=VMEM_LIMIT_BYTES),
        name="mixer_sample",
    )(x, mod4, gain, w_in, w_dw, b_dw, ln_g, ln_b, w_pool, pool_scale, w_out, state_conv, state_pool)


def kernel(x_prompt, x_sample, state_conv, state_pool, c_prompt, c_sample, w_ada, b_ada, g_ffn1, w_ffn1_up, w_ffn1_down, g_mix, w_in, w_dw, b_dw, ln_g, ln_b, w_pool, pool_scale, w_out, g_ffn2, w_ffn2_up, w_ffn2_down, g_final):
    depth = w_ada.shape[0]
    assert depth == 1 and state_conv.shape[0] == 1 and state_pool.shape[0] == 1
    bp, tp, d = x_prompt.shape
    bs, ts, _ = x_sample.shape

    mod_p, mod_s, (w1u, w1d) = _adaln(c_prompt, c_sample, w_ada[0], b_ada[0], col_block=d, cast_slabs=8,
                                      casts=(w_ffn1_up[0], w_ffn1_down[0]))
    mod_p = mod_p.reshape(N_MOD, bp, 1, d)
    mod_s = mod_s.reshape(N_MOD, 1, bs, d)

    g_final2 = g_final.reshape(1, d)
    n_groups, group, _ = w_pool.shape[1:]

    tile_p = min(tp, 1024)
    grid_p = (bp, tp // tile_p)
    x_spec_p = pl.BlockSpec((1, tile_p, d), lambda i, j: (i, j, 0))
    mod_spec_p = lambda mb: pl.BlockSpec((3, 1, 1, d), lambda i, j: (mb, i, 0, 0))
    seq_s = min(bs, 64)
    grid_s = (bs // seq_s,)
    x_spec_s = pl.BlockSpec((ts, seq_s, d), lambda i: (0, i, 0))
    mod_spec_s = lambda mb: pl.BlockSpec((3, 1, seq_s, d), lambda i: (mb, 0, i, 0))
    xs = jnp.transpose(x_sample, (1, 0, 2))
    sconv = jnp.transpose(state_conv, (0, 2, 1, 3))
    spool = jnp.transpose(state_pool, (0, 2, 1, 3))

    later = (w_ffn2_up[0], w_ffn2_down[0], w_in[0], w_out[0], w_pool[0].reshape(n_groups * group, group))
    h_p, (w2u, w2d, w_in_b, w_out_b, w_pool_b) = _ffn(
        x_prompt, x_spec_p, mod_p, mod_spec_p(0), grid_p, g_ffn1, w1u, w1d, None, casts=later)
    mixer_w = (g_mix, w_in_b, w_dw[0], b_dw, ln_g, ln_b, w_pool_b.reshape(n_groups, group, group),
               pool_scale, w_out_b)
    h_s, _ = _ffn(xs, x_spec_s, mod_s, mod_spec_s(0), grid_s, g_ffn1, w1u, w1d, None)

    h_p, conv_p, pool_p = _mixer_prompt(h_p, mod_p, *mixer_w, time_block=tile_p, sub_block=min(tile_p, 512))
    h_s, conv_s, pool_s = _mixer_sample(h_s, mod_s, *mixer_w, sconv, spool, batch_block=min(bs, 32))

    y_p, _ = _ffn(h_p, x_spec_p, mod_p, mod_spec_p(2), grid_p, g_ffn2, w2u, w2d, g_final2)
    y_s, _ = _ffn(h_s, x_spec_s, mod_s, mod_spec_s(2), grid_s, g_ffn2, w2u, w2d, g_final2)
    return (y_p, jnp.transpose(y_s, (1, 0, 2)), conv_p, pool_p,
            jnp.transpose(conv_s, (0, 2, 1, 3)), jnp.transpose(pool_s, (0, 2, 1, 3)))
```

```python
import functools

import jax
import jax.numpy as jnp
from jax import lax
from jax.experimental import pallas as pl
from jax.experimental.pallas import tpu as pltpu

EPS = 1e-6
HALF = 0.5
N_MOD = 9
CONV_WIDTH = 31
POOL_WINDOWS = (2, 4, 8, 16)
POOL_MAX = max(POOL_WINDOWS)
SAMPLE_POS0 = 16384

LANES = 128
BF16_SUBLANES = 16
FIRST_CHUNK_SLABS = 4
CONV_HALO = 32
POOL_HALO = 16
VMEM_LIMIT_BYTES = 56 * 1024 * 1024

F32 = jnp.float32
BF16 = jnp.bfloat16


def _dot(a, b):
    return jnp.dot(a, b, preferred_element_type=F32)


def _sigmoid(x):
    return 1.0 / (1.0 + jnp.exp(-x))


def _rms_scale(x):
    return lax.rsqrt(jnp.mean(x * x, axis=-1, keepdims=True) + EPS)


def _const_spec(shape):
    zeros = (0,) * len(shape)
    return pl.BlockSpec(shape, lambda *_: zeros, pipeline_mode=pl.Buffered(1))


def _cast_specs(casts, n_slabs, slab_of_step):
    specs = []
    for w in casts:
        rows, cols = w.shape
        assert rows % (n_slabs * BF16_SUBLANES) == 0, (w.shape, n_slabs)
        specs.append(pl.BlockSpec((rows // n_slabs, cols), lambda *idx: (slab_of_step(*idx), 0)))
    return specs


def _cast_slabs(cast_in, cast_out):
    for src, dst in zip(cast_in, cast_out):
        dst[...] = src[...].astype(BF16)


def _adaln_kernel(cp_ref, cs_ref, w_ref, b_ref, *rest, n_casts):
    cast_in, (op_ref, os_ref), cast_out = rest[:n_casts], rest[n_casts:n_casts + 2], rest[n_casts + 2:]
    w = w_ref[...].astype(BF16)
    b = b_ref[0]

    def mod(c):
        s = c * _sigmoid(c)
        return _dot(s.astype(BF16), w) + b

    op_ref[0] = mod(cp_ref[...])
    os_ref[0] = mod(cs_ref[...])
    _cast_slabs(cast_in, cast_out)


def _adaln(c_prompt, c_sample, w_ada, b_ada, *, col_block, cast_slabs, casts=()):
    bp, d = c_prompt.shape
    bs = c_sample.shape[0]
    per_mod = d // col_block
    n_steps = N_MOD * per_mod
    assert d % col_block == 0 and n_steps >= cast_slabs
    cast_specs = _cast_specs(casts, cast_slabs, lambda j: jnp.minimum(j, cast_slabs - 1))
    outs = pl.pallas_call(
        functools.partial(_adaln_kernel, n_casts=len(casts)),
        grid=(n_steps,),
        in_specs=[
            pl.BlockSpec((bp, d), lambda j: (0, 0)),
            pl.BlockSpec((bs, d), lambda j: (0, 0)),
            pl.BlockSpec((d, col_block), lambda j: (0, j)),
            pl.BlockSpec((1, 1, col_block), lambda j: (j, 0, 0)),
        ] + cast_specs,
        out_specs=[
            pl.BlockSpec((1, bp, col_block), lambda j: (j // per_mod, 0, j % per_mod)),
            pl.BlockSpec((1, bs, col_block), lambda j: (j // per_mod, 0, j % per_mod)),
        ] + cast_specs,
        out_shape=[
            jax.ShapeDtypeStruct((N_MOD, bp, d), F32),
            jax.ShapeDtypeStruct((N_MOD, bs, d), F32),
        ] + [jax.ShapeDtypeStruct(w.shape, BF16) for w in casts],
        compiler_params=pltpu.CompilerParams(
            dimension_semantics=("arbitrary",), vmem_limit_bytes=VMEM_LIMIT_BYTES),
        name="adaln",
    )(c_prompt, c_sample, w_ada, b_ada.reshape(n_steps, 1, col_block), *casts)
    return outs[0], outs[1], outs[2:]


def _ffn_in(x, mod, gain, xb_scr, r0=0):
    shift, scale, _ = mod
    xn = (x * _rms_scale(x)) * gain * (1.0 + scale) + shift
    m = xn.size // xn.shape[-1]
    xb_scr[r0:r0 + m, :] = xn.reshape(m, xn.shape[-1]).astype(BF16)


def _ffn_chunk(lo, ff_chunk, wup_ref, xb_scr, a_scr, rows=slice(None)):
    d_ff = a_scr.shape[1]
    xb = xb_scr[rows, :]
    g = _dot(xb, wup_ref[:, lo:lo + ff_chunk])
    u = _dot(xb, wup_ref[:, d_ff + lo:d_ff + lo + ff_chunk])
    a_scr[rows, lo:lo + ff_chunk] = (g * _sigmoid(g) * u).astype(BF16)


def _ffn_out(x, mod, wdn_ref, a_scr, g_final):
    gate = mod[2]
    out = x + (HALF * gate) * _dot(a_scr[...], wdn_ref[...]).reshape(x.shape)
    if g_final is not None:
        out = (out * _rms_scale(out)) * g_final
    return out


def _ffn_kernel(x_ref, mod_ref, g_ref, wup_ref, wdn_ref, *rest, ff_chunk, final_norm, n_casts):
    rest = list(rest)
    gf_ref = rest.pop(0) if final_norm else None
    cast_in = [rest.pop(0) for _ in range(n_casts)]
    o_ref = rest.pop(0)
    cast_out = [rest.pop(0) for _ in range(n_casts)]
    xb_scr, a_scr = rest
    mod = (mod_ref[0], mod_ref[1], mod_ref[2])
    lead, rows, _ = x_ref.shape
    if lead == 1 and rows % (FIRST_CHUNK_SLABS * BF16_SUBLANES) == 0:
        slab = rows // FIRST_CHUNK_SLABS
        for r0 in range(0, rows, slab):
            _ffn_in(x_ref[:, r0:r0 + slab, :], mod, g_ref[...], xb_scr, r0)
            _ffn_chunk(0, ff_chunk, wup_ref, xb_scr, a_scr, slice(r0, r0 + slab))
        first = ff_chunk
    else:
        _ffn_in(x_ref[...], mod, g_ref[...], xb_scr)
        first = 0
    for lo in range(first, a_scr.shape[1], ff_chunk):
        _ffn_chunk(lo, ff_chunk, wup_ref, xb_scr, a_scr)
    o_ref[...] = _ffn_out(x_ref[...], mod, wdn_ref, a_scr, gf_ref[...] if final_norm else None)
    _cast_slabs(cast_in, cast_out)


def _ffn(x, x_spec, mod4, mod_spec, grid, gain, w_up, w_down, g_final, *, ff_chunk=256, casts=()):
    d = x.shape[-1]
    d_ff = w_down.shape[0]
    m = x_spec.block_shape[0] * x_spec.block_shape[1]
    final_norm = g_final is not None
    in_specs = [x_spec, mod_spec, _const_spec((1, d)), _const_spec((d, 2 * d_ff)), _const_spec((d_ff, d))]
    args = [x, mod4, gain, w_up, w_down]
    if final_norm:
        in_specs.append(_const_spec((1, d)))
        args.append(g_final)
    n_steps = 1
    for g in grid:
        n_steps *= g

    def flat_step(*idx):
        s = idx[0]
        for i, g in zip(idx[1:], grid[1:]):
            s = s * g + i
        return s

    cast_specs = _cast_specs(casts, n_steps, flat_step)
    outs = pl.pallas_call(
        functools.partial(_ffn_kernel, ff_chunk=ff_chunk, final_norm=final_norm, n_casts=len(casts)),
        grid=grid,
        in_specs=in_specs + cast_specs,
        out_specs=[x_spec] + cast_specs,
        out_shape=[jax.ShapeDtypeStruct(x.shape, F32)] + [jax.ShapeDtypeStruct(w.shape, BF16) for w in casts],
        scratch_shapes=[pltpu.VMEM((m, d), BF16), pltpu.VMEM((m, d_ff), BF16)],
        compiler_params=pltpu.CompilerParams(
            dimension_semantics=("arbitrary",) * len(grid), vmem_limit_bytes=VMEM_LIMIT_BYTES),
        name="ffn_final" if final_norm else "ffn",
    )(*args, *casts)
    return outs[0], outs[1:]


def _layernorm_silu(v, ln_g, ln_b):
    mu = jnp.mean(v, axis=-1, keepdims=True)
    c = v - mu
    var = jnp.mean(c * c, axis=-1, keepdims=True)
    y = (c * lax.rsqrt(var + EPS)) * ln_g + ln_b
    return y * _sigmoid(y)


def _mixer_rows_in(x, mod, gain, win_ref, cs, ps, r0):
    sb = x.shape[0]
    n_slabs = cs.shape[0]
    d_conv = n_slabs * LANES
    shift, scale, _ = mod
    hm = (x * _rms_scale(x)) * gain * (1.0 + scale) + shift
    proj = _dot(hm.astype(BF16), win_ref[...])
    u_conv = proj[:, :d_conv] * _sigmoid(proj[:, d_conv:2 * d_conv])
    for s in range(n_slabs):
        cs[s, CONV_HALO + r0:CONV_HALO + r0 + sb, :] = u_conv[:, s * LANES:(s + 1) * LANES]
        ps[s, POOL_HALO + r0:POOL_HALO + r0 + sb, :] = (
            proj[:, 2 * d_conv + s * LANES:2 * d_conv + (s + 1) * LANES])


def _mixer_rows_out(x, mod, pos0, r0, row_block, wdw_ref, bdw_ref, lng_ref, lnb_ref,
                    wpool_ref, pscale_ref, wout_ref, cs, ps, v_scr, mix_scr):
    sb = x.shape[0]
    n_slabs = cs.shape[0]
    d_conv = n_slabs * LANES
    assert len(POOL_WINDOWS) == ps.shape[0] and wpool_ref.shape[1] == LANES
    gate = mod[2]
    rows = slice(r0, r0 + sb)

    first = CONV_HALO - (CONV_WIDTH - 1)
    for s in range(n_slabs):
        lanes = slice(s * LANES, (s + 1) * LANES)
        for r in range(r0, r0 + sb, row_block):
            acc = jnp.broadcast_to(bdw_ref[:, lanes], (row_block, LANES))
            for k in range(CONV_WIDTH):
                acc = acc + wdw_ref[k:k + 1, lanes] * cs[s, first + k + r:first + k + r + row_block, :]
            v_scr[r:r + row_block, lanes] = acc
    mix_scr[rows, 0:d_conv] = _layernorm_silu(v_scr[rows, :], lng_ref[...], lnb_ref[...]).astype(BF16)

    pos = pos0 + r0 + lax.broadcasted_iota(jnp.int32, (sb, 1), 0)
    for gi, w in enumerate(POOL_WINDOWS):
        lanes = slice(gi * LANES, (gi + 1) * LANES)
        cur = ps[gi, POOL_HALO + r0:POOL_HALO + r0 + sb, :]
        win_sum = cur
        for j in range(1, w):
            win_sum = win_sum + ps[gi, POOL_HALO + r0 - j:POOL_HALO + r0 - j + sb, :]
        count = jnp.minimum(pos + 1, w).astype(F32)
        pooled = win_sum / count - cur
        mixed = _dot(pooled.astype(BF16), wpool_ref[gi])
        mix_scr[rows, d_conv + gi * LANES:d_conv + (gi + 1) * LANES] = (
            mixed * pscale_ref[:, lanes]).astype(BF16)

    return x + gate * _dot(mix_scr[rows, :], wout_ref[...])


def _mixer_prompt_kernel(x_ref, mod_ref, g_ref, win_ref, wdw_ref, bdw_ref, lng_ref, lnb_ref,
                         wpool_ref, pscale_ref, wout_ref,
                         o_ref, oconv_ref, opool_ref,
                         cs, ps, v_scr, mix_scr, *, sub_block, row_block):
    t_idx = pl.program_id(1)
    n_t = pl.num_programs(1)
    tm = x_ref.shape[1]
    n_slabs = cs.shape[0]

    @pl.when(t_idx == 0)
    def _():
        cs[:, 0:CONV_HALO, :] = jnp.zeros((n_slabs, CONV_HALO, LANES), F32)
        ps[:, 0:POOL_HALO, :] = jnp.zeros((ps.shape[0], POOL_HALO, LANES), F32)

    mod = (mod_ref[0, 0], mod_ref[1, 0], mod_ref[2, 0])
    _mixer_rows_in(x_ref[0], mod, g_ref[...], win_ref, cs, ps, 0)
    for r0 in range(0, tm, sub_block):
        x = x_ref[0, r0:r0 + sub_block, :]
        o_ref[0, r0:r0 + sub_block, :] = _mixer_rows_out(
            x, mod, t_idx * tm, r0, row_block, wdw_ref, bdw_ref,
            lng_ref, lnb_ref, wpool_ref, pscale_ref, wout_ref, cs, ps, v_scr, mix_scr)

    @pl.when(t_idx == n_t - 1)
    def _():
        for s in range(n_slabs):
            lanes = slice(s * LANES, (s + 1) * LANES)
            oconv_ref[0, 0, :, lanes] = cs[s, CONV_HALO + tm - (CONV_WIDTH - 1):CONV_HALO + tm, :]
            opool_ref[0, 0, :, lanes] = ps[s, POOL_HALO + tm - (POOL_MAX - 1):POOL_HALO + tm, :]

    cs[:, 0:CONV_HALO, :] = cs[:, tm:tm + CONV_HALO, :]
    ps[:, 0:POOL_HALO, :] = ps[:, tm:tm + POOL_HALO, :]


def _mixer_prompt(x, mod4, gain, w_in, w_dw, b_dw, ln_g, ln_b, w_pool, pool_scale, w_out,
                  *, time_block, sub_block, row_block=16):
    b, t, d = x.shape
    d_conv = w_dw.shape[1]
    d_pool = pool_scale.shape[1]
    n_groups, group = w_pool.shape[0], w_pool.shape[1]
    assert group == LANES and d_conv % LANES == 0 and d_pool == n_groups * LANES
    assert time_block % sub_block == 0 and sub_block % row_block == 0
    x_spec = pl.BlockSpec((1, time_block, d), lambda i, j: (i, j, 0))
    return pl.pallas_call(
        functools.partial(_mixer_prompt_kernel, sub_block=sub_block, row_block=row_block),
        grid=(b, t // time_block),
        in_specs=[
            x_spec,
            pl.BlockSpec((3, 1, 1, d), lambda i, j: (1, i, 0, 0)),
            _const_spec((1, d)),
            _const_spec((d, 2 * d_conv + d_pool)),
            _const_spec((CONV_WIDTH, d_conv)),
            _const_spec((1, d_conv)),
            _const_spec((1, d_conv)),
            _const_spec((1, d_conv)),
            _const_spec((n_groups, group, group)),
            _const_spec((1, d_pool)),
            _const_spec((d_conv + d_pool, d)),
        ],
        out_specs=[
            x_spec,
            pl.BlockSpec((1, 1, CONV_WIDTH - 1, d_conv), lambda i, j: (0, i, 0, 0)),
            pl.BlockSpec((1, 1, POOL_MAX - 1, d_pool), lambda i, j: (0, i, 0, 0)),
        ],
        out_shape=[
            jax.ShapeDtypeStruct(x.shape, F32),
            jax.ShapeDtypeStruct((1, b, CONV_WIDTH - 1, d_conv), F32),
            jax.ShapeDtypeStruct((1, b, POOL_MAX - 1, d_pool), F32),
        ],
        scratch_shapes=[
            pltpu.VMEM((d_conv // LANES, CONV_HALO + time_block, LANES), F32),
            pltpu.VMEM((n_groups, POOL_HALO + time_block, LANES), F32),
            pltpu.VMEM((time_block, d_conv), F32),
            pltpu.VMEM((time_block, d_conv + d_pool), BF16),
        ],
        compiler_params=pltpu.CompilerParams(
            dimension_semantics=("arbitrary", "arbitrary"), vmem_limit_bytes=VMEM_LIMIT_BYTES),
        name="mixer_prompt",
    )(x, mod4, gain, w_in, w_dw, b_dw, ln_g, ln_b, w_pool, pool_scale, w_out)


def _mixer_sample_kernel(x_ref, mod_ref, g_ref, win_ref, wdw_ref, bdw_ref, lng_ref, lnb_ref,
                         wpool_ref, pscale_ref, wout_ref, sconv_ref, spool_ref,
                         o_ref, oconv_ref, opool_ref, mix_scr):
    tt, bb, d = x_ref.shape
    m = tt * bb
    n_conv, d_conv = sconv_ref.shape[1], sconv_ref.shape[3]
    n_pool = spool_ref.shape[1]

    x = x_ref[...]
    shift, scale, gate = mod_ref[0], mod_ref[1], mod_ref[2]
    hm = (x * _rms_scale(x)) * g_ref[...] * (1.0 + scale) + shift
    proj = _dot(hm.reshape(m, d).astype(BF16), win_ref[...])
    u_conv = proj[:, :d_conv] * _sigmoid(proj[:, d_conv:2 * d_conv])
    u_pool = proj[:, 2 * d_conv:]

    def conv_in(s):
        return sconv_ref[0, s] if s < n_conv else u_conv[(s - n_conv) * bb:(s - n_conv + 1) * bb, :]

    def pool_in(s, lanes=slice(None)):
        return spool_ref[0, s, :, lanes] if s < n_pool else u_pool[(s - n_pool) * bb:(s - n_pool + 1) * bb, lanes]

    for s in range(n_conv):
        oconv_ref[0, s] = conv_in(s + tt)
    for s in range(n_pool):
        opool_ref[0, s] = pool_in(s + tt)

    for t in range(tt):
        acc = jnp.broadcast_to(bdw_ref[...], (bb, d_conv))
        for k in range(CONV_WIDTH):
            acc = acc + wdw_ref[k:k + 1, :] * conv_in(t + k)
        mix_scr[t * bb:(t + 1) * bb, 0:d_conv] = _layernorm_silu(acc, lng_ref[...], lnb_ref[...]).astype(BF16)

    for gi, w in enumerate(POOL_WINDOWS):
        lanes = slice(gi * LANES, (gi + 1) * LANES)
        count = float(min(SAMPLE_POS0 + 1, w))
        for t in range(tt):
            cur = pool_in(n_pool + t, lanes)
            win_sum = cur
            for j in range(1, w):
                win_sum = win_sum + pool_in(n_pool + t - j, lanes)
            pooled = win_sum / count - cur
            mixed = _dot(pooled.astype(BF16), wpool_ref[gi])
            mix_scr[t * bb:(t + 1) * bb, d_conv + gi * LANES:d_conv + (gi + 1) * LANES] = (
                mixed * pscale_ref[:, lanes]).astype(BF16)

    o_ref[...] = x + gate * _dot(mix_scr[...], wout_ref[...]).reshape(tt, bb, d)


def _mixer_sample(x, mod4, gain, w_in, w_dw, b_dw, ln_g, ln_b, w_pool, pool_scale, w_out,
                  state_conv, state_pool, *, batch_block):
    t, b, d = x.shape
    d_conv = w_dw.shape[1]
    d_pool = pool_scale.shape[1]
    n_groups, group = w_pool.shape[0], w_pool.shape[1]
    assert group == LANES and d_pool == n_groups * LANES and d_conv == d_pool
    x_spec = pl.BlockSpec((t, batch_block, d), lambda i: (0, i, 0))
    conv_spec = pl.BlockSpec((1, CONV_WIDTH - 1, batch_block, d_conv), lambda i: (0, 0, i, 0))
    pool_spec = pl.BlockSpec((1, POOL_MAX - 1, batch_block, d_pool), lambda i: (0, 0, i, 0))
    return pl.pallas_call(
        _mixer_sample_kernel,
        grid=(b // batch_block,),
        in_specs=[
            x_spec,
            pl.BlockSpec((3, 1, batch_block, d), lambda i: (1, 0, i, 0)),
            _const_spec((1, d)),
            _const_spec((d, 2 * d_conv + d_pool)),
            _const_spec((CONV_WIDTH, d_conv)),
            _const_spec((1, d_conv)),
            _const_spec((1, d_conv)),
            _const_spec((1, d_conv)),
            _const_spec((n_groups, group, group)),
            _const_spec((1, d_pool)),
            _const_spec((d_conv + d_pool, d)),
            conv_spec,
            pool_spec,
        ],
        out_specs=[x_spec, conv_spec, pool_spec],
        out_shape=[
            jax.ShapeDtypeStruct(x.shape, F32),
            jax.ShapeDtypeStruct(state_conv.shape, F32),
            jax.ShapeDtypeStruct(state_pool.shape, F32),
        ],
        scratch_shapes=[pltpu.VMEM((t * batch_block, d_conv + d_pool), BF16)],
        compiler_params=pltpu.CompilerParams(
            dimension_semantics=("arbitrary",), vmem_limit_bytes=VMEM_LIMIT_BYTES),
        name="mixer_sample",
    )(x, mod4, gain, w_in, w_dw, b_dw, ln_g, ln_b, w_pool, pool_scale, w_out, state_conv, state_pool)


def kernel(x_prompt, x_sample, state_conv, state_pool, c_prompt, c_sample, w_ada, b_ada, g_ffn1, w_ffn1_up, w_ffn1_down, g_mix, w_in, w_dw, b_dw, ln_g, ln_b, w_pool, pool_scale, w_out, g_ffn2, w_ffn2_up, w_ffn2_down, g_final):
    depth = w_ada.shape[0]
    assert depth == 1 and state_conv.shape[0] == 1 and state_pool.shape[0] == 1
    bp, tp, d = x_prompt.shape
    bs, ts, _ = x_sample.shape

    mod_p, mod_s, (w1u, w1d) = _adaln(c_prompt, c_sample, w_ada[0], b_ada[0], col_block=d, cast_slabs=8,
                                      casts=(w_ffn1_up[0], w_ffn1_down[0]))
    mod_p = mod_p.reshape(N_MOD, bp, 1, d)
    mod_s = mod_s.reshape(N_MOD, 1, bs, d)

    g_final2 = g_final.reshape(1, d)
    n_groups, group, _ = w_pool.shape[1:]

    tile_p = min(tp, 1024)
    grid_p = (bp, tp // tile_p)
    x_spec_p = pl.BlockSpec((1, tile_p, d), lambda i, j: (i, j, 0))
    mod_spec_p = lambda mb: pl.BlockSpec((3, 1, 1, d), lambda i, j: (mb, i, 0, 0))
    seq_s = min(bs, 64)
    grid_s = (bs // seq_s,)
    x_spec_s = pl.BlockSpec((ts, seq_s, d), lambda i: (0, i, 0))
    mod_spec_s = lambda mb: pl.BlockSpec((3, 1, seq_s, d), lambda i: (mb, 0, i, 0))
    xs = jnp.transpose(x_sample, (1, 0, 2))
    sconv = jnp.transpose(state_conv, (0, 2, 1, 3))
    spool = jnp.transpose(state_pool, (0, 2, 1, 3))

    later = (w_ffn2_up[0], w_ffn2_down[0], w_in[0], w_out[0], w_pool[0].reshape(n_groups * group, group))
    h_p, (w2u, w2d, w_in_b, w_out_b, w_pool_b) = _ffn(
        x_prompt, x_spec_p, mod_p, mod_spec_p(0), grid_p, g_ffn1, w1u, w1d, None, casts=later)
    mixer_w = (g_mix, w_in_b, w_dw[0], b_dw, ln_g, ln_b, w_pool_b.reshape(n_groups, group, group),
               pool_scale, w_out_b)
    h_s, _ = _ffn(xs, x_spec_s, mod_s, mod_spec_s(0), grid_s, g_ffn1, w1u, w1d, None)

    h_p, conv_p, pool_p = _mixer_prompt(h_p, mod_p, *mixer_w, time_block=tile_p, sub_block=min(tile_p, 256))
    h_s, conv_s, pool_s = _mixer_sample(h_s, mod_s, *mixer_w, sconv, spool, batch_block=min(bs, 32))

    y_p, _ = _ffn(h_p, x_spec_p, mod_p, mod_spec_p(2), grid_p, g_ffn2, w2u, w2d, g_final2)
    y_s, _ = _ffn(h_s, x_spec_s, mod_s, mod_spec_s(2), grid_s, g_ffn2, w2u, w2d, g_final2)
    return (y_p, jnp.transpose(y_s, (1, 0, 2)), conv_p, pool_p,
            jnp.transpose(conv_s, (0, 2, 1, 3)), jnp.transpose(pool_s, (0, 2, 1, 3)))
```

```python
import functools

import jax
import jax.numpy as jnp
from jax import lax
from jax.experimental import pallas as pl
from jax.experimental.pallas import tpu as pltpu

EPS = 1e-6
HALF = 0.5
N_MOD = 9
CONV_WIDTH = 31
POOL_WINDOWS = (2, 4, 8, 16)
POOL_MAX = max(POOL_WINDOWS)
SAMPLE_POS0 = 16384

LANES = 128
BF16_SUBLANES = 16
FIRST_CHUNK_SLABS = 4
CONV_HALO = 32
POOL_HALO = 16
VMEM_LIMIT_BYTES = 56 * 1024 * 1024

F32 = jnp.float32
BF16 = jnp.bfloat16


def _dot(a, b):
    return jnp.dot(a, b, preferred_element_type=F32)


def _sigmoid(x):
    return 1.0 / (1.0 + jnp.exp(-x))


def _rms_scale(x):
    return lax.rsqrt(jnp.mean(x * x, axis=-1, keepdims=True) + EPS)


def _const_spec(shape):
    zeros = (0,) * len(shape)
    return pl.BlockSpec(shape, lambda *_: zeros, pipeline_mode=pl.Buffered(1))


def _cast_specs(casts, n_slabs, slab_of_step):
    specs = []
    for w in casts:
        rows, cols = w.shape
        assert rows % (n_slabs * BF16_SUBLANES) == 0, (w.shape, n_slabs)
        specs.append(pl.BlockSpec((rows // n_slabs, cols), lambda *idx: (slab_of_step(*idx), 0)))
    return specs


def _cast_slabs(cast_in, cast_out):
    for src, dst in zip(cast_in, cast_out):
        dst[...] = src[...].astype(BF16)


def _adaln_kernel(cp_ref, cs_ref, w_ref, b_ref, *rest, n_casts):
    cast_in, (op_ref, os_ref), cast_out = rest[:n_casts], rest[n_casts:n_casts + 2], rest[n_casts + 2:]
    w = w_ref[...].astype(BF16)
    b = b_ref[0]

    def mod(c):
        s = c * _sigmoid(c)
        return _dot(s.astype(BF16), w) + b

    op_ref[0] = mod(cp_ref[...])
    os_ref[0] = mod(cs_ref[...])
    _cast_slabs(cast_in, cast_out)


def _adaln(c_prompt, c_sample, w_ada, b_ada, *, col_block, cast_slabs, casts=()):
    bp, d = c_prompt.shape
    bs = c_sample.shape[0]
    per_mod = d // col_block
    n_steps = N_MOD * per_mod
    assert d % col_block == 0 and n_steps >= cast_slabs
    cast_specs = _cast_specs(casts, cast_slabs, lambda j: jnp.minimum(j, cast_slabs - 1))
    outs = pl.pallas_call(
        functools.partial(_adaln_kernel, n_casts=len(casts)),
        grid=(n_steps,),
        in_specs=[
            pl.BlockSpec((bp, d), lambda j: (0, 0)),
            pl.BlockSpec((bs, d), lambda j: (0, 0)),
            pl.BlockSpec((d, col_block), lambda j: (0, j)),
            pl.BlockSpec((1, 1, col_block), lambda j: (j, 0, 0)),
        ] + cast_specs,
        out_specs=[
            pl.BlockSpec((1, bp, col_block), lambda j: (j // per_mod, 0, j % per_mod)),
            pl.BlockSpec((1, bs, col_block), lambda j: (j // per_mod, 0, j % per_mod)),
        ] + cast_specs,
        out_shape=[
            jax.ShapeDtypeStruct((N_MOD, bp, d), F32),
            jax.ShapeDtypeStruct((N_MOD, bs, d), F32),
        ] + [jax.ShapeDtypeStruct(w.shape, BF16) for w in casts],
        compiler_params=pltpu.CompilerParams(
            dimension_semantics=("arbitrary",), vmem_limit_bytes=VMEM_LIMIT_BYTES),
        name="adaln",
    )(c_prompt, c_sample, w_ada, b_ada.reshape(n_steps, 1, col_block), *casts)
    return outs[0], outs[1], outs[2:]


def _ffn_in(x, mod, gain, xb_scr, r0=0):
    shift, scale, _ = mod
    xn = (x * _rms_scale(x)) * gain * (1.0 + scale) + shift
    m = xn.size // xn.shape[-1]
    xb_scr[r0:r0 + m, :] = xn.reshape(m, xn.shape[-1]).astype(BF16)


def _ffn_chunk(lo, ff_chunk, wup_ref, xb_scr, a_scr, rows=slice(None)):
    d_ff = a_scr.shape[1]
    xb = xb_scr[rows, :]
    g = _dot(xb, wup_ref[:, lo:lo + ff_chunk])
    u = _dot(xb, wup_ref[:, d_ff + lo:d_ff + lo + ff_chunk])
    a_scr[rows, lo:lo + ff_chunk] = (g * _sigmoid(g) * u).astype(BF16)


def _ffn_out(x, mod, wdn_ref, a_scr, g_final, rows=slice(None)):
    gate = mod[2]
    out = x + (HALF * gate) * _dot(a_scr[rows, :], wdn_ref[...]).reshape(x.shape)
    if g_final is not None:
        out = (out * _rms_scale(out)) * g_final
    return out


def _ffn_block(x_ref, mod_ref, o_ref, gain, wup_ref, wdn_ref, xb_scr, a_scr, g_final, ff_chunk):
    mod = (mod_ref[0], mod_ref[1], mod_ref[2])
    lead, rows, _ = x_ref.shape
    used = slice(0, lead * rows)
    if lead == 1 and rows % (FIRST_CHUNK_SLABS * BF16_SUBLANES) == 0:
        slab = rows // FIRST_CHUNK_SLABS
        for r0 in range(0, rows, slab):
            _ffn_in(x_ref[:, r0:r0 + slab, :], mod, gain, xb_scr, r0)
            _ffn_chunk(0, ff_chunk, wup_ref, xb_scr, a_scr, slice(r0, r0 + slab))
        first = ff_chunk
    else:
        _ffn_in(x_ref[...], mod, gain, xb_scr)
        first = 0
    for lo in range(first, a_scr.shape[1], ff_chunk):
        _ffn_chunk(lo, ff_chunk, wup_ref, xb_scr, a_scr, used)
    o_ref[...] = _ffn_out(x_ref[...], mod, wdn_ref, a_scr, g_final, used)


def _ffn_kernel(xp_ref, modp_ref, xs_ref, mods_ref, g_ref, wup_ref, wdn_ref, *rest,
                ff_chunk, final_norm, n_casts, prompt_steps):
    rest = list(rest)
    gf_ref = rest.pop(0) if final_norm else None
    cast_in = [rest.pop(0) for _ in range(n_casts)]
    op_ref, os_ref = rest.pop(0), rest.pop(0)
    cast_out = [rest.pop(0) for _ in range(n_casts)]
    xb_scr, a_scr = rest
    step = pl.program_id(0)
    g_final = gf_ref[...] if final_norm else None

    @pl.when(step < prompt_steps)
    def _():
        _ffn_block(xp_ref, modp_ref, op_ref, g_ref[...], wup_ref, wdn_ref, xb_scr, a_scr, g_final, ff_chunk)

    @pl.when(step >= prompt_steps)
    def _():
        _ffn_block(xs_ref, mods_ref, os_ref, g_ref[...], wup_ref, wdn_ref, xb_scr, a_scr, g_final, ff_chunk)

    _cast_slabs(cast_in, cast_out)


def _ffn(xp, xs, mod_p, mod_s, mod_block, gain, w_up, w_down, g_final, *, tile_p, seq_s, ff_chunk=256, casts=()):
    bp, tp, d = xp.shape
    ts, bs, _ = xs.shape
    d_ff = w_down.shape[0]
    tiles = tp // tile_p
    prompt_steps = bp * tiles
    sample_steps = bs // seq_s
    final_norm = g_final is not None

    def p_tile(s):
        s = jnp.minimum(s, prompt_steps - 1)
        return s // tiles, s % tiles

    def s_block(s):
        return jnp.maximum(s - prompt_steps, 0)

    xp_spec = pl.BlockSpec((1, tile_p, d), lambda s: (*p_tile(s), 0))
    xs_spec = pl.BlockSpec((ts, seq_s, d), lambda s: (0, s_block(s), 0))
    in_specs = [
        xp_spec,
        pl.BlockSpec((3, 1, 1, d), lambda s: (mod_block, p_tile(s)[0], 0, 0)),
        xs_spec,
        pl.BlockSpec((3, 1, seq_s, d), lambda s: (mod_block, 0, s_block(s), 0)),
        _const_spec((1, d)), _const_spec((d, 2 * d_ff)), _const_spec((d_ff, d)),
    ]
    args = [xp, mod_p, xs, mod_s, gain, w_up, w_down]
    if final_norm:
        in_specs.append(_const_spec((1, d)))
        args.append(g_final)
    cast_specs = _cast_specs(casts, prompt_steps, lambda s: jnp.minimum(s, prompt_steps - 1))
    m = max(tile_p, ts * seq_s)
    outs = pl.pallas_call(
        functools.partial(_ffn_kernel, ff_chunk=ff_chunk, final_norm=final_norm, n_casts=len(casts),
                          prompt_steps=prompt_steps),
        grid=(prompt_steps + sample_steps,),
        in_specs=in_specs + cast_specs,
        out_specs=[xp_spec, xs_spec] + cast_specs,
        out_shape=[jax.ShapeDtypeStruct(xp.shape, F32), jax.ShapeDtypeStruct(xs.shape, F32)]
        + [jax.ShapeDtypeStruct(w.shape, BF16) for w in casts],
        scratch_shapes=[pltpu.VMEM((m, d), BF16), pltpu.VMEM((m, d_ff), BF16)],
        compiler_params=pltpu.CompilerParams(
            dimension_semantics=("arbitrary",), vmem_limit_bytes=VMEM_LIMIT_BYTES),
        name="ffn_final" if final_norm else "ffn",
    )(*args, *casts)
    return outs[0], outs[1], outs[2:]


def _layernorm_silu(v, ln_g, ln_b):
    mu = jnp.mean(v, axis=-1, keepdims=True)
    c = v - mu
    var = jnp.mean(c * c, axis=-1, keepdims=True)
    y = (c * lax.rsqrt(var + EPS)) * ln_g + ln_b
    return y * _sigmoid(y)


def _mixer_rows_in(x, mod, gain, win_ref, cs, ps, r0):
    sb = x.shape[0]
    n_slabs = cs.shape[0]
    d_conv = n_slabs * LANES
    shift, scale, _ = mod
    hm = (x * _rms_scale(x)) * gain * (1.0 + scale) + shift
    proj = _dot(hm.astype(BF16), win_ref[...])
    u_conv = proj[:, :d_conv] * _sigmoid(proj[:, d_conv:2 * d_conv])
    for s in range(n_slabs):
        cs[s, CONV_HALO + r0:CONV_HALO + r0 + sb, :] = u_conv[:, s * LANES:(s + 1) * LANES]
        ps[s, POOL_HALO + r0:POOL_HALO + r0 + sb, :] = (
            proj[:, 2 * d_conv + s * LANES:2 * d_conv + (s + 1) * LANES])


def _mixer_rows_out(x, mod, pos0, r0, row_block, wdw_ref, bdw_ref, lng_ref, lnb_ref,
                    wpool_ref, pscale_ref, wout_ref, cs, ps, v_scr, mix_scr):
    sb = x.shape[0]
    n_slabs = cs.shape[0]
    d_conv = n_slabs * LANES
    assert len(POOL_WINDOWS) == ps.shape[0] and wpool_ref.shape[1] == LANES
    gate = mod[2]
    rows = slice(r0, r0 + sb)

    first = CONV_HALO - (CONV_WIDTH - 1)
    for s in range(n_slabs):
        lanes = slice(s * LANES, (s + 1) * LANES)
        for r in range(r0, r0 + sb, row_block):
            acc = jnp.broadcast_to(bdw_ref[:, lanes], (row_block, LANES))
            for k in range(CONV_WIDTH):
                acc = acc + wdw_ref[k:k + 1, lanes] * cs[s, first + k + r:first + k + r + row_block, :]
            v_scr[r:r + row_block, lanes] = acc
    mix_scr[rows, 0:d_conv] = _layernorm_silu(v_scr[rows, :], lng_ref[...], lnb_ref[...]).astype(BF16)

    pos = pos0 + r0 + lax.broadcasted_iota(jnp.int32, (sb, 1), 0)
    for gi, w in enumerate(POOL_WINDOWS):
        lanes = slice(gi * LANES, (gi + 1) * LANES)
        cur = ps[gi, POOL_HALO + r0:POOL_HALO + r0 + sb, :]
        win_sum = cur
        for j in range(1, w):
            win_sum = win_sum + ps[gi, POOL_HALO + r0 - j:POOL_HALO + r0 - j + sb, :]
        count = jnp.minimum(pos + 1, w).astype(F32)
        pooled = win_sum / count - cur
        mixed = _dot(pooled.astype(BF16), wpool_ref[gi])
        mix_scr[rows, d_conv + gi * LANES:d_conv + (gi + 1) * LANES] = (
            mixed * pscale_ref[:, lanes]).astype(BF16)

    return x + gate * _dot(mix_scr[rows, :], wout_ref[...])


def _mixer_prompt_kernel(x_ref, mod_ref, g_ref, win_ref, wdw_ref, bdw_ref, lng_ref, lnb_ref,
                         wpool_ref, pscale_ref, wout_ref, *rest, sub_block, row_block, n_casts):
    cast_in, (o_ref, oconv_ref, opool_ref) = rest[:n_casts], rest[n_casts:n_casts + 3]
    cast_out, (cs, ps, v_scr, mix_scr) = rest[n_casts + 3:2 * n_casts + 3], rest[2 * n_casts + 3:]
    t_idx = pl.program_id(1)
    n_t = pl.num_programs(1)
    tm = x_ref.shape[1]
    n_slabs = cs.shape[0]

    @pl.when(t_idx == 0)
    def _():
        cs[:, 0:CONV_HALO, :] = jnp.zeros((n_slabs, CONV_HALO, LANES), F32)
        ps[:, 0:POOL_HALO, :] = jnp.zeros((ps.shape[0], POOL_HALO, LANES), F32)

    mod = (mod_ref[0, 0], mod_ref[1, 0], mod_ref[2, 0])
    _mixer_rows_in(x_ref[0], mod, g_ref[...], win_ref, cs, ps, 0)
    for r0 in range(0, tm, sub_block):
        x = x_ref[0, r0:r0 + sub_block, :]
        o_ref[0, r0:r0 + sub_block, :] = _mixer_rows_out(
            x, mod, t_idx * tm, r0, row_block, wdw_ref, bdw_ref,
            lng_ref, lnb_ref, wpool_ref, pscale_ref, wout_ref, cs, ps, v_scr, mix_scr)

    @pl.when(t_idx == n_t - 1)
    def _():
        for s in range(n_slabs):
            lanes = slice(s * LANES, (s + 1) * LANES)
            oconv_ref[0, 0, :, lanes] = cs[s, CONV_HALO + tm - (CONV_WIDTH - 1):CONV_HALO + tm, :]
            opool_ref[0, 0, :, lanes] = ps[s, POOL_HALO + tm - (POOL_MAX - 1):POOL_HALO + tm, :]

    cs[:, 0:CONV_HALO, :] = cs[:, tm:tm + CONV_HALO, :]
    ps[:, 0:POOL_HALO, :] = ps[:, tm:tm + POOL_HALO, :]
    _cast_slabs(cast_in, cast_out)


def _mixer_prompt(x, mod4, gain, w_in, w_dw, b_dw, ln_g, ln_b, w_pool, pool_scale, w_out,
                  *, time_block, sub_block, row_block=16, casts=()):
    b, t, d = x.shape
    d_conv = w_dw.shape[1]
    d_pool = pool_scale.shape[1]
    n_groups, group = w_pool.shape[0], w_pool.shape[1]
    assert group == LANES and d_conv % LANES == 0 and d_pool == n_groups * LANES
    assert time_block % sub_block == 0 and sub_block % row_block == 0
    x_spec = pl.BlockSpec((1, time_block, d), lambda i, j: (i, j, 0))
    tiles = t // time_block
    cast_specs = _cast_specs(casts, b * tiles, lambda i, j: i * tiles + j)
    outs = pl.pallas_call(
        functools.partial(_mixer_prompt_kernel, sub_block=sub_block, row_block=row_block, n_casts=len(casts)),
        grid=(b, tiles),
        in_specs=[
            x_spec,
            pl.BlockSpec((3, 1, 1, d), lambda i, j: (1, i, 0, 0)),
            _const_spec((1, d)),
            _const_spec((d, 2 * d_conv + d_pool)),
            _const_spec((CONV_WIDTH, d_conv)),
            _const_spec((1, d_conv)),
            _const_spec((1, d_conv)),
            _const_spec((1, d_conv)),
            _const_spec((n_groups, group, group)),
            _const_spec((1, d_pool)),
            _const_spec((d_conv + d_pool, d)),
        ] + cast_specs,
        out_specs=[
            x_spec,
            pl.BlockSpec((1, 1, CONV_WIDTH - 1, d_conv), lambda i, j: (0, i, 0, 0)),
            pl.BlockSpec((1, 1, POOL_MAX - 1, d_pool), lambda i, j: (0, i, 0, 0)),
        ] + cast_specs,
        out_shape=[
            jax.ShapeDtypeStruct(x.shape, F32),
            jax.ShapeDtypeStruct((1, b, CONV_WIDTH - 1, d_conv), F32),
            jax.ShapeDtypeStruct((1, b, POOL_MAX - 1, d_pool), F32),
        ] + [jax.ShapeDtypeStruct(w.shape, BF16) for w in casts],
        scratch_shapes=[
            pltpu.VMEM((d_conv // LANES, CONV_HALO + time_block, LANES), F32),
            pltpu.VMEM((n_groups, POOL_HALO + time_block, LANES), F32),
            pltpu.VMEM((time_block, d_conv), F32),
            pltpu.VMEM((time_block, d_conv + d_pool), BF16),
        ],
        compiler_params=pltpu.CompilerParams(
            dimension_semantics=("arbitrary", "arbitrary"), vmem_limit_bytes=VMEM_LIMIT_BYTES),
        name="mixer_prompt",
    )(x, mod4, gain, w_in, w_dw, b_dw, ln_g, ln_b, w_pool, pool_scale, w_out, *casts)
    return outs[0], outs[1], outs[2], outs[3:]


def _mixer_sample_kernel(x_ref, mod_ref, g_ref, win_ref, wdw_ref, bdw_ref, lng_ref, lnb_ref,
                         wpool_ref, pscale_ref, wout_ref, sconv_ref, spool_ref,
                         o_ref, oconv_ref, opool_ref, mix_scr):
    tt, bb, d = x_ref.shape
    m = tt * bb
    n_conv, d_conv = sconv_ref.shape[1], sconv_ref.shape[3]
    n_pool = spool_ref.shape[1]

    x = x_ref[...]
    shift, scale, gate = mod_ref[0], mod_ref[1], mod_ref[2]
    hm = (x * _rms_scale(x)) * g_ref[...] * (1.0 + scale) + shift
    proj = _dot(hm.reshape(m, d).astype(BF16), win_ref[...])
    u_conv = proj[:, :d_conv] * _sigmoid(proj[:, d_conv:2 * d_conv])
    u_pool = proj[:, 2 * d_conv:]

    def conv_in(s):
        return sconv_ref[0, s] if s < n_conv else u_conv[(s - n_conv) * bb:(s - n_conv + 1) * bb, :]

    def pool_in(s, lanes=slice(None)):
        return spool_ref[0, s, :, lanes] if s < n_pool else u_pool[(s - n_pool) * bb:(s - n_pool + 1) * bb, lanes]

    for s in range(n_conv):
        oconv_ref[0, s] = conv_in(s + tt)
    for s in range(n_pool):
        opool_ref[0, s] = pool_in(s + tt)

    for t in range(tt):
        acc = jnp.broadcast_to(bdw_ref[...], (bb, d_conv))
        for k in range(CONV_WIDTH):
            acc = acc + wdw_ref[k:k + 1, :] * conv_in(t + k)
        mix_scr[t * bb:(t + 1) * bb, 0:d_conv] = _layernorm_silu(acc, lng_ref[...], lnb_ref[...]).astype(BF16)

    for gi, w in enumerate(POOL_WINDOWS):
        lanes = slice(gi * LANES, (gi + 1) * LANES)
        count = float(min(SAMPLE_POS0 + 1, w))
        for t in range(tt):
            cur = pool_in(n_pool + t, lanes)
            win_sum = cur
            for j in range(1, w):
                win_sum = win_sum + pool_in(n_pool + t - j, lanes)
            pooled = win_sum / count - cur
            mixed = _dot(pooled.astype(BF16), wpool_ref[gi])
            mix_scr[t * bb:(t + 1) * bb, d_conv + gi * LANES:d_conv + (gi + 1) * LANES] = (
                mixed * pscale_ref[:, lanes]).astype(BF16)

    o_ref[...] = x + gate * _dot(mix_scr[...], wout_ref[...]).reshape(tt, bb, d)


def _mixer_sample(x, mod4, gain, w_in, w_dw, b_dw, ln_g, ln_b, w_pool, pool_scale, w_out,
                  state_conv, state_pool, *, batch_block):
    t, b, d = x.shape
    d_conv = w_dw.shape[1]
    d_pool = pool_scale.shape[1]
    n_groups, group = w_pool.shape[0], w_pool.shape[1]
    assert group == LANES and d_pool == n_groups * LANES and d_conv == d_pool
    x_spec = pl.BlockSpec((t, batch_block, d), lambda i: (0, i, 0))
    conv_spec = pl.BlockSpec((1, CONV_WIDTH - 1, batch_block, d_conv), lambda i: (0, 0, i, 0))
    pool_spec = pl.BlockSpec((1, POOL_MAX - 1, batch_block, d_pool), lambda i: (0, 0, i, 0))
    return pl.pallas_call(
        _mixer_sample_kernel,
        grid=(b // batch_block,),
        in_specs=[
            x_spec,
            pl.BlockSpec((3, 1, batch_block, d), lambda i: (1, 0, i, 0)),
            _const_spec((1, d)),
            _const_spec((d, 2 * d_conv + d_pool)),
            _const_spec((CONV_WIDTH, d_conv)),
            _const_spec((1, d_conv)),
            _const_spec((1, d_conv)),
            _const_spec((1, d_conv)),
            _const_spec((n_groups, group, group)),
            _const_spec((1, d_pool)),
            _const_spec((d_conv + d_pool, d)),
            conv_spec,
            pool_spec,
        ],
        out_specs=[x_spec, conv_spec, pool_spec],
        out_shape=[
            jax.ShapeDtypeStruct(x.shape, F32),
            jax.ShapeDtypeStruct(state_conv.shape, F32),
            jax.ShapeDtypeStruct(state_pool.shape, F32),
        ],
        scratch_shapes=[pltpu.VMEM((t * batch_block, d_conv + d_pool), BF16)],
        compiler_params=pltpu.CompilerParams(
            dimension_semantics=("arbitrary",), vmem_limit_bytes=VMEM_LIMIT_BYTES),
        name="mixer_sample",
    )(x, mod4, gain, w_in, w_dw, b_dw, ln_g, ln_b, w_pool, pool_scale, w_out, state_conv, state_pool)


def kernel(x_prompt, x_sample, state_conv, state_pool, c_prompt, c_sample, w_ada, b_ada, g_ffn1, w_ffn1_up, w_ffn1_down, g_mix, w_in, w_dw, b_dw, ln_g, ln_b, w_pool, pool_scale, w_out, g_ffn2, w_ffn2_up, w_ffn2_down, g_final):
    depth = w_ada.shape[0]
    assert depth == 1 and state_conv.shape[0] == 1 and state_pool.shape[0] == 1
    bp, tp, d = x_prompt.shape
    bs, ts, _ = x_sample.shape

    mod_p, mod_s, (w1u, w1d) = _adaln(c_prompt, c_sample, w_ada[0], b_ada[0], col_block=d, cast_slabs=8,
                                      casts=(w_ffn1_up[0], w_ffn1_down[0]))
    mod_p = mod_p.reshape(N_MOD, bp, 1, d)
    mod_s = mod_s.reshape(N_MOD, 1, bs, d)

    g_final2 = g_final.reshape(1, d)
    n_groups, group, _ = w_pool.shape[1:]

    tiles = dict(tile_p=min(tp, 1024), seq_s=min(bs, 64))
    xs = jnp.transpose(x_sample, (1, 0, 2))
    sconv = jnp.transpose(state_conv, (0, 2, 1, 3))
    spool = jnp.transpose(state_pool, (0, 2, 1, 3))

    later = (w_in[0], w_out[0], w_pool[0].reshape(n_groups * group, group))
    h_p, h_s, (w_in_b, w_out_b, w_pool_b) = _ffn(
        x_prompt, xs, mod_p, mod_s, 0, g_ffn1, w1u, w1d, None, casts=later, **tiles)
    mixer_w = (g_mix, w_in_b, w_dw[0], b_dw, ln_g, ln_b, w_pool_b.reshape(n_groups, group, group),
               pool_scale, w_out_b)

    h_p, conv_p, pool_p, (w2u, w2d) = _mixer_prompt(
        h_p, mod_p, *mixer_w, time_block=min(tp, 1024), sub_block=min(tp, 256),
        casts=(w_ffn2_up[0], w_ffn2_down[0]))
    h_s, conv_s, pool_s = _mixer_sample(h_s, mod_s, *mixer_w, sconv, spool, batch_block=min(bs, 64))

    y_p, y_s, _ = _ffn(h_p, h_s, mod_p, mod_s, 2, g_ffn2, w2u, w2d, g_final2, **tiles)
    return (y_p, jnp.transpose(y_s, (1, 0, 2)), conv_p, pool_p,
            jnp.transpose(conv_s, (0, 2, 1, 3)), jnp.transpose(pool_s, (0, 2, 1, 3)))
```

```python
import functools

import jax
import jax.numpy as jnp
from jax import lax
from jax.experimental import pallas as pl
from jax.experimental.pallas import tpu as pltpu

EPS = 1e-6
HALF = 0.5
N_MOD = 9
CONV_WIDTH = 31
POOL_WINDOWS = (2, 4, 8, 16)
POOL_MAX = max(POOL_WINDOWS)
SAMPLE_POS0 = 16384

LANES = 128
BF16_SUBLANES = 16
FIRST_CHUNK_SLABS = 4
CONV_HALO = 32
POOL_HALO = 16
VMEM_LIMIT_BYTES = 56 * 1024 * 1024

F32 = jnp.float32
BF16 = jnp.bfloat16


def _dot(a, b):
    return jnp.dot(a, b, preferred_element_type=F32)


def _sigmoid(x):
    return 1.0 / (1.0 + jnp.exp(-x))


def _rms_scale(x):
    return lax.rsqrt(jnp.mean(x * x, axis=-1, keepdims=True) + EPS)


def _const_spec(shape):
    zeros = (0,) * len(shape)
    return pl.BlockSpec(shape, lambda *_: zeros, pipeline_mode=pl.Buffered(1))


def _cast_specs(casts, n_slabs, slab_of_step):
    specs = []
    for w in casts:
        rows, cols = w.shape
        assert rows % (n_slabs * BF16_SUBLANES) == 0, (w.shape, n_slabs)
        specs.append(pl.BlockSpec((rows // n_slabs, cols), lambda *idx: (slab_of_step(*idx), 0)))
    return specs


def _cast_slabs(cast_in, cast_out):
    for src, dst in zip(cast_in, cast_out):
        dst[...] = src[...].astype(BF16)


def _adaln_kernel(cp_ref, cs_ref, w_ref, b_ref, *rest, n_casts):
    cast_in, (op_ref, os_ref), cast_out = rest[:n_casts], rest[n_casts:n_casts + 2], rest[n_casts + 2:]
    w = w_ref[...].astype(BF16)
    b = b_ref[0]

    def mod(c):
        s = c * _sigmoid(c)
        return _dot(s.astype(BF16), w) + b

    op_ref[0] = mod(cp_ref[...])
    os_ref[0] = mod(cs_ref[...])
    _cast_slabs(cast_in, cast_out)


def _adaln(c_prompt, c_sample, w_ada, b_ada, *, col_block, cast_slabs, casts=()):
    bp, d = c_prompt.shape
    bs = c_sample.shape[0]
    per_mod = d // col_block
    n_steps = N_MOD * per_mod
    assert d % col_block == 0 and n_steps >= cast_slabs
    cast_specs = _cast_specs(casts, cast_slabs, lambda j: jnp.minimum(j, cast_slabs - 1))
    outs = pl.pallas_call(
        functools.partial(_adaln_kernel, n_casts=len(casts)),
        grid=(n_steps,),
        in_specs=[
            pl.BlockSpec((bp, d), lambda j: (0, 0)),
            pl.BlockSpec((bs, d), lambda j: (0, 0)),
            pl.BlockSpec((d, col_block), lambda j: (0, j)),
            pl.BlockSpec((1, 1, col_block), lambda j: (j, 0, 0)),
        ] + cast_specs,
        out_specs=[
            pl.BlockSpec((1, bp, col_block), lambda j: (j // per_mod, 0, j % per_mod)),
            pl.BlockSpec((1, bs, col_block), lambda j: (j // per_mod, 0, j % per_mod)),
        ] + cast_specs,
        out_shape=[
            jax.ShapeDtypeStruct((N_MOD, bp, d), F32),
            jax.ShapeDtypeStruct((N_MOD, bs, d), F32),
        ] + [jax.ShapeDtypeStruct(w.shape, BF16) for w in casts],
        compiler_params=pltpu.CompilerParams(
            dimension_semantics=("arbitrary",), vmem_limit_bytes=VMEM_LIMIT_BYTES),
        name="adaln",
    )(c_prompt, c_sample, w_ada, b_ada.reshape(n_steps, 1, col_block), *casts)
    return outs[0], outs[1], outs[2:]


def _ffn_in(x, mod, gain, xb_scr, r0=0):
    shift, scale, _ = mod
    xn = (x * _rms_scale(x)) * gain * (1.0 + scale) + shift
    m = xn.size // xn.shape[-1]
    xb_scr[r0:r0 + m, :] = xn.reshape(m, xn.shape[-1]).astype(BF16)


def _ffn_chunk(lo, ff_chunk, wup_ref, xb_scr, a_scr, rows=slice(None)):
    d_ff = a_scr.shape[1]
    xb = xb_scr[rows, :]
    g = _dot(xb, wup_ref[:, lo:lo + ff_chunk])
    u = _dot(xb, wup_ref[:, d_ff + lo:d_ff + lo + ff_chunk])
    a_scr[rows, lo:lo + ff_chunk] = (g * _sigmoid(g) * u).astype(BF16)


def _ffn_out(x, mod, wdn_ref, a_scr, g_final, rows=slice(None)):
    gate = mod[2]
    out = x + (HALF * gate) * _dot(a_scr[rows, :], wdn_ref[...]).reshape(x.shape)
    if g_final is not None:
        out = (out * _rms_scale(out)) * g_final
    return out


def _ffn_block(x_ref, mod_ref, o_ref, gain, wup_ref, wdn_ref, xb_scr, a_scr, g_final, ff_chunk):
    mod = (mod_ref[0], mod_ref[1], mod_ref[2])
    lead, rows, _ = x_ref.shape
    used = slice(0, lead * rows)
    if lead == 1 and rows % (FIRST_CHUNK_SLABS * BF16_SUBLANES) == 0:
        slab = rows // FIRST_CHUNK_SLABS
        for r0 in range(0, rows, slab):
            _ffn_in(x_ref[:, r0:r0 + slab, :], mod, gain, xb_scr, r0)
            _ffn_chunk(0, ff_chunk, wup_ref, xb_scr, a_scr, slice(r0, r0 + slab))
        first = ff_chunk
    else:
        _ffn_in(x_ref[...], mod, gain, xb_scr)
        first = 0
    for lo in range(first, a_scr.shape[1], ff_chunk):
        _ffn_chunk(lo, ff_chunk, wup_ref, xb_scr, a_scr, used)
    o_ref[...] = _ffn_out(x_ref[...], mod, wdn_ref, a_scr, g_final, used)


def _ffn_kernel(xp_ref, modp_ref, xs_ref, mods_ref, g_ref, wup_ref, wdn_ref, *rest,
                ff_chunk, final_norm, n_casts, prompt_steps):
    rest = list(rest)
    gf_ref = rest.pop(0) if final_norm else None
    cast_in = [rest.pop(0) for _ in range(n_casts)]
    op_ref, os_ref = rest.pop(0), rest.pop(0)
    cast_out = [rest.pop(0) for _ in range(n_casts)]
    xb_scr, a_scr = rest
    step = pl.program_id(0)
    g_final = gf_ref[...] if final_norm else None

    @pl.when(step < prompt_steps)
    def _():
        _ffn_block(xp_ref, modp_ref, op_ref, g_ref[...], wup_ref, wdn_ref, xb_scr, a_scr, g_final, ff_chunk)

    @pl.when(step >= prompt_steps)
    def _():
        _ffn_block(xs_ref, mods_ref, os_ref, g_ref[...], wup_ref, wdn_ref, xb_scr, a_scr, g_final, ff_chunk)

    _cast_slabs(cast_in, cast_out)


def _ffn(xp, xs, mod_p, mod_s, mod_block, gain, w_up, w_down, g_final, *, tile_p, seq_s, ff_chunk=256, casts=()):
    bp, tp, d = xp.shape
    ts, bs, _ = xs.shape
    d_ff = w_down.shape[0]
    tiles = tp // tile_p
    prompt_steps = bp * tiles
    sample_steps = bs // seq_s
    final_norm = g_final is not None

    def p_tile(s):
        s = jnp.minimum(s, prompt_steps - 1)
        return s // tiles, s % tiles

    def s_block(s):
        return jnp.maximum(s - prompt_steps, 0)

    xp_spec = pl.BlockSpec((1, tile_p, d), lambda s: (*p_tile(s), 0))
    xs_spec = pl.BlockSpec((ts, seq_s, d), lambda s: (0, s_block(s), 0))
    in_specs = [
        xp_spec,
        pl.BlockSpec((3, 1, 1, d), lambda s: (mod_block, p_tile(s)[0], 0, 0)),
        xs_spec,
        pl.BlockSpec((3, 1, seq_s, d), lambda s: (mod_block, 0, s_block(s), 0)),
        _const_spec((1, d)), _const_spec((d, 2 * d_ff)), _const_spec((d_ff, d)),
    ]
    args = [xp, mod_p, xs, mod_s, gain, w_up, w_down]
    if final_norm:
        in_specs.append(_const_spec((1, d)))
        args.append(g_final)
    cast_specs = _cast_specs(casts, prompt_steps, lambda s: jnp.minimum(s, prompt_steps - 1))
    m = max(tile_p, ts * seq_s)
    outs = pl.pallas_call(
        functools.partial(_ffn_kernel, ff_chunk=ff_chunk, final_norm=final_norm, n_casts=len(casts),
                          prompt_steps=prompt_steps),
        grid=(prompt_steps + sample_steps,),
        in_specs=in_specs + cast_specs,
        out_specs=[xp_spec, xs_spec] + cast_specs,
        out_shape=[jax.ShapeDtypeStruct(xp.shape, F32), jax.ShapeDtypeStruct(xs.shape, F32)]
        + [jax.ShapeDtypeStruct(w.shape, BF16) for w in casts],
        scratch_shapes=[pltpu.VMEM((m, d), BF16), pltpu.VMEM((m, d_ff), BF16)],
        compiler_params=pltpu.CompilerParams(
            dimension_semantics=("arbitrary",), vmem_limit_bytes=VMEM_LIMIT_BYTES),
        name="ffn_final" if final_norm else "ffn",
    )(*args, *casts)
    return outs[0], outs[1], outs[2:]


def _layernorm_silu(v, ln_g, ln_b):
    mu = jnp.mean(v, axis=-1, keepdims=True)
    c = v - mu
    var = jnp.mean(c * c, axis=-1, keepdims=True)
    y = (c * lax.rsqrt(var + EPS)) * ln_g + ln_b
    return y * _sigmoid(y)


def _mixer_rows_in(x, mod, gain, win_ref, cs, ps, r0):
    sb = x.shape[0]
    n_slabs = cs.shape[0]
    d_conv = n_slabs * LANES
    shift, scale, _ = mod
    hm = (x * _rms_scale(x)) * gain * (1.0 + scale) + shift
    proj = _dot(hm.astype(BF16), win_ref[...])
    u_conv = proj[:, :d_conv] * _sigmoid(proj[:, d_conv:2 * d_conv])
    for s in range(n_slabs):
        cs[s, CONV_HALO + r0:CONV_HALO + r0 + sb, :] = u_conv[:, s * LANES:(s + 1) * LANES]
        ps[s, POOL_HALO + r0:POOL_HALO + r0 + sb, :] = (
            proj[:, 2 * d_conv + s * LANES:2 * d_conv + (s + 1) * LANES])


def _mixer_rows_out(x, mod, pos0, r0, row_block, wdw_ref, bdw_ref, lng_ref, lnb_ref,
                    wpool_ref, pscale_ref, wout_ref, cs, ps, v_scr, mix_scr):
    sb = x.shape[0]
    n_slabs = cs.shape[0]
    d_conv = n_slabs * LANES
    assert len(POOL_WINDOWS) == ps.shape[0] and wpool_ref.shape[1] == LANES
    gate = mod[2]
    rows = slice(r0, r0 + sb)

    first = CONV_HALO - (CONV_WIDTH - 1)
    for s in range(n_slabs):
        lanes = slice(s * LANES, (s + 1) * LANES)
        for r in range(r0, r0 + sb, row_block):
            acc = jnp.broadcast_to(bdw_ref[:, lanes], (row_block, LANES))
            for k in range(CONV_WIDTH):
                acc = acc + wdw_ref[k:k + 1, lanes] * cs[s, first + k + r:first + k + r + row_block, :]
            v_scr[r:r + row_block, lanes] = acc
    mix_scr[rows, 0:d_conv] = _layernorm_silu(v_scr[rows, :], lng_ref[...], lnb_ref[...]).astype(BF16)

    pos = pos0 + r0 + lax.broadcasted_iota(jnp.int32, (sb, 1), 0)
    for gi, w in enumerate(POOL_WINDOWS):
        lanes = slice(gi * LANES, (gi + 1) * LANES)
        cur = ps[gi, POOL_HALO + r0:POOL_HALO + r0 + sb, :]
        win_sum = cur
        for j in range(1, w):
            win_sum = win_sum + ps[gi, POOL_HALO + r0 - j:POOL_HALO + r0 - j + sb, :]
        count = jnp.minimum(pos + 1, w).astype(F32)
        pooled = win_sum / count - cur
        mixed = _dot(pooled.astype(BF16), wpool_ref[gi])
        mix_scr[rows, d_conv + gi * LANES:d_conv + (gi + 1) * LANES] = (
            mixed * pscale_ref[:, lanes]).astype(BF16)

    return x + gate * _dot(mix_scr[rows, :], wout_ref[...])


def _mixer_prompt_kernel(x_ref, mod_ref, g_ref, win_ref, wdw_ref, bdw_ref, lng_ref, lnb_ref,
                         wpool_ref, pscale_ref, wout_ref, *rest, sub_block, row_block, n_casts):
    cast_in, (o_ref, oconv_ref, opool_ref) = rest[:n_casts], rest[n_casts:n_casts + 3]
    cast_out, (cs, ps, v_scr, mix_scr) = rest[n_casts + 3:2 * n_casts + 3], rest[2 * n_casts + 3:]
    t_idx = pl.program_id(1)
    n_t = pl.num_programs(1)
    tm = x_ref.shape[1]
    n_slabs = cs.shape[0]

    @pl.when(t_idx == 0)
    def _():
        cs[:, 0:CONV_HALO, :] = jnp.zeros((n_slabs, CONV_HALO, LANES), F32)
        ps[:, 0:POOL_HALO, :] = jnp.zeros((ps.shape[0], POOL_HALO, LANES), F32)

    mod = (mod_ref[0, 0], mod_ref[1, 0], mod_ref[2, 0])
    _mixer_rows_in(x_ref[0], mod, g_ref[...], win_ref, cs, ps, 0)
    for r0 in range(0, tm, sub_block):
        x = x_ref[0, r0:r0 + sub_block, :]
        o_ref[0, r0:r0 + sub_block, :] = _mixer_rows_out(
            x, mod, t_idx * tm, r0, row_block, wdw_ref, bdw_ref,
            lng_ref, lnb_ref, wpool_ref, pscale_ref, wout_ref, cs, ps, v_scr, mix_scr)

    @pl.when(t_idx == n_t - 1)
    def _():
        for s in range(n_slabs):
            lanes = slice(s * LANES, (s + 1) * LANES)
            oconv_ref[0, 0, :, lanes] = cs[s, CONV_HALO + tm - (CONV_WIDTH - 1):CONV_HALO + tm, :]
            opool_ref[0, 0, :, lanes] = ps[s, POOL_HALO + tm - (POOL_MAX - 1):POOL_HALO + tm, :]

    cs[:, 0:CONV_HALO, :] = cs[:, tm:tm + CONV_HALO, :]
    ps[:, 0:POOL_HALO, :] = ps[:, tm:tm + POOL_HALO, :]
    _cast_slabs(cast_in, cast_out)


def _mixer_prompt(x, mod4, gain, w_in, w_dw, b_dw, ln_g, ln_b, w_pool, pool_scale, w_out,
                  *, time_block, sub_block, row_block=16, casts=()):
    b, t, d = x.shape
    d_conv = w_dw.shape[1]
    d_pool = pool_scale.shape[1]
    n_groups, group = w_pool.shape[0], w_pool.shape[1]
    assert group == LANES and d_conv % LANES == 0 and d_pool == n_groups * LANES
    assert time_block % sub_block == 0 and sub_block % row_block == 0
    x_spec = pl.BlockSpec((1, time_block, d), lambda i, j: (i, j, 0))
    tiles = t // time_block
    cast_specs = _cast_specs(casts, b * tiles, lambda i, j: i * tiles + j)
    outs = pl.pallas_call(
        functools.partial(_mixer_prompt_kernel, sub_block=sub_block, row_block=row_block, n_casts=len(casts)),
        grid=(b, tiles),
        in_specs=[
            x_spec,
            pl.BlockSpec((3, 1, 1, d), lambda i, j: (1, i, 0, 0)),
            _const_spec((1, d)),
            _const_spec((d, 2 * d_conv + d_pool)),
            _const_spec((CONV_WIDTH, d_conv)),
            _const_spec((1, d_conv)),
            _const_spec((1, d_conv)),
            _const_spec((1, d_conv)),
            _const_spec((n_groups, group, group)),
            _const_spec((1, d_pool)),
            _const_spec((d_conv + d_pool, d)),
        ] + cast_specs,
        out_specs=[
            x_spec,
            pl.BlockSpec((1, 1, CONV_WIDTH - 1, d_conv), lambda i, j: (0, i, 0, 0)),
            pl.BlockSpec((1, 1, POOL_MAX - 1, d_pool), lambda i, j: (0, i, 0, 0)),
        ] + cast_specs,
        out_shape=[
            jax.ShapeDtypeStruct(x.shape, F32),
            jax.ShapeDtypeStruct((1, b, CONV_WIDTH - 1, d_conv), F32),
            jax.ShapeDtypeStruct((1, b, POOL_MAX - 1, d_pool), F32),
        ] + [jax.ShapeDtypeStruct(w.shape, BF16) for w in casts],
        scratch_shapes=[
            pltpu.VMEM((d_conv // LANES, CONV_HALO + time_block, LANES), F32),
            pltpu.VMEM((n_groups, POOL_HALO + time_block, LANES), F32),
            pltpu.VMEM((time_block, d_conv), F32),
            pltpu.VMEM((time_block, d_conv + d_pool), BF16),
        ],
        compiler_params=pltpu.CompilerParams(
            dimension_semantics=("arbitrary", "arbitrary"), vmem_limit_bytes=VMEM_LIMIT_BYTES),
        name="mixer_prompt",
    )(x, mod4, gain, w_in, w_dw, b_dw, ln_g, ln_b, w_pool, pool_scale, w_out, *casts)
    return outs[0], outs[1], outs[2], outs[3:]


def _mixer_sample_kernel(x_ref, mod_ref, g_ref, win_ref, wdw_ref, bdw_ref, lng_ref, lnb_ref,
                         wpool_ref, pscale_ref, wout_ref, sconv_ref, spool_ref,
                         o_ref, oconv_ref, opool_ref, mix_scr):
    tt, bb, d = x_ref.shape
    m = tt * bb
    n_conv, d_conv = sconv_ref.shape[1], sconv_ref.shape[3]
    n_pool = spool_ref.shape[1]

    x = x_ref[...]
    shift, scale, gate = mod_ref[0], mod_ref[1], mod_ref[2]
    hm = (x * _rms_scale(x)) * g_ref[...] * (1.0 + scale) + shift
    proj = _dot(hm.reshape(m, d).astype(BF16), win_ref[...])
    u_conv = proj[:, :d_conv] * _sigmoid(proj[:, d_conv:2 * d_conv])
    u_pool = proj[:, 2 * d_conv:]

    def conv_in(s):
        return sconv_ref[0, s] if s < n_conv else u_conv[(s - n_conv) * bb:(s - n_conv + 1) * bb, :]

    def pool_in(s, lanes=slice(None)):
        return spool_ref[0, s, :, lanes] if s < n_pool else u_pool[(s - n_pool) * bb:(s - n_pool + 1) * bb, lanes]

    for s in range(n_conv):
        oconv_ref[0, s] = conv_in(s + tt)
    for s in range(n_pool):
        opool_ref[0, s] = pool_in(s + tt)

    for t in range(tt):
        acc = jnp.broadcast_to(bdw_ref[...], (bb, d_conv))
        for k in range(CONV_WIDTH):
            acc = acc + wdw_ref[k:k + 1, :] * conv_in(t + k)
        mix_scr[t * bb:(t + 1) * bb, 0:d_conv] = _layernorm_silu(acc, lng_ref[...], lnb_ref[...]).astype(BF16)

    for gi, w in enumerate(POOL_WINDOWS):
        lanes = slice(gi * LANES, (gi + 1) * LANES)
        count = float(min(SAMPLE_POS0 + 1, w))
        for t in range(tt):
            cur = pool_in(n_pool + t, lanes)
            win_sum = cur
            for j in range(1, w):
                win_sum = win_sum + pool_in(n_pool + t - j, lanes)
            pooled = win_sum / count - cur
            mixed = _dot(pooled.astype(BF16), wpool_ref[gi])
            mix_scr[t * bb:(t + 1) * bb, d_conv + gi * LANES:d_conv + (gi + 1) * LANES] = (
                mixed * pscale_ref[:, lanes]).astype(BF16)

    o_ref[...] = x + gate * _dot(mix_scr[...], wout_ref[...]).reshape(tt, bb, d)


def _mixer_sample(x, mod4, gain, w_in, w_dw, b_dw, ln_g, ln_b, w_pool, pool_scale, w_out,
                  state_conv, state_pool, *, batch_block):
    t, b, d = x.shape
    d_conv = w_dw.shape[1]
    d_pool = pool_scale.shape[1]
    n_groups, group = w_pool.shape[0], w_pool.shape[1]
    assert group == LANES and d_pool == n_groups * LANES and d_conv == d_pool
    x_spec = pl.BlockSpec((t, batch_block, d), lambda i: (0, i, 0))
    conv_spec = pl.BlockSpec((1, CONV_WIDTH - 1, batch_block, d_conv), lambda i: (0, 0, i, 0))
    pool_spec = pl.BlockSpec((1, POOL_MAX - 1, batch_block, d_pool), lambda i: (0, 0, i, 0))
    return pl.pallas_call(
        _mixer_sample_kernel,
        grid=(b // batch_block,),
        in_specs=[
            x_spec,
            pl.BlockSpec((3, 1, batch_block, d), lambda i: (1, 0, i, 0)),
            _const_spec((1, d)),
            _const_spec((d, 2 * d_conv + d_pool)),
            _const_spec((CONV_WIDTH, d_conv)),
            _const_spec((1, d_conv)),
            _const_spec((1, d_conv)),
            _const_spec((1, d_conv)),
            _const_spec((n_groups, group, group)),
            _const_spec((1, d_pool)),
            _const_spec((d_conv + d_pool, d)),
            conv_spec,
            pool_spec,
        ],
        out_specs=[x_spec, conv_spec, pool_spec],
        out_shape=[
            jax.ShapeDtypeStruct(x.shape, F32),
            jax.ShapeDtypeStruct(state_conv.shape, F32),
            jax.ShapeDtypeStruct(state_pool.shape, F32),
        ],
        scratch_shapes=[pltpu.VMEM((t * batch_block, d_conv + d_pool), BF16)],
        compiler_params=pltpu.CompilerParams(
            dimension_semantics=("arbitrary",), vmem_limit_bytes=VMEM_LIMIT_BYTES),
        name="mixer_sample",
    )(x, mod4, gain, w_in, w_dw, b_dw, ln_g, ln_b, w_pool, pool_scale, w_out, state_conv, state_pool)


def kernel(x_prompt, x_sample, state_conv, state_pool, c_prompt, c_sample, w_ada, b_ada, g_ffn1, w_ffn1_up, w_ffn1_down, g_mix, w_in, w_dw, b_dw, ln_g, ln_b, w_pool, pool_scale, w_out, g_ffn2, w_ffn2_up, w_ffn2_down, g_final):
    depth = w_ada.shape[0]
    assert depth == 1 and state_conv.shape[0] == 1 and state_pool.shape[0] == 1
    bp, tp, d = x_prompt.shape
    bs, ts, _ = x_sample.shape

    mod_p, mod_s, (w1u, w1d) = _adaln(c_prompt, c_sample, w_ada[0], b_ada[0], col_block=d, cast_slabs=8,
                                      casts=(w_ffn1_up[0], w_ffn1_down[0]))
    mod_p = mod_p.reshape(N_MOD, bp, 1, d)
    mod_s = mod_s.reshape(N_MOD, 1, bs, d)

    g_final2 = g_final.reshape(1, d)
    n_groups, group, _ = w_pool.shape[1:]

    tiles = dict(tile_p=min(tp, 1024), seq_s=min(bs, 64))
    xs = jnp.transpose(x_sample, (1, 0, 2))
    sconv = jnp.transpose(state_conv, (0, 2, 1, 3))
    spool = jnp.transpose(state_pool, (0, 2, 1, 3))

    later = (w_in[0], w_out[0], w_pool[0].reshape(n_groups * group, group))
    h_p, h_s, (w_in_b, w_out_b, w_pool_b) = _ffn(
        x_prompt, xs, mod_p, mod_s, 0, g_ffn1, w1u, w1d, None, casts=later, **tiles)
    mixer_w = (g_mix, w_in_b, w_dw[0], b_dw, ln_g, ln_b, w_pool_b.reshape(n_groups, group, group),
               pool_scale, w_out_b)

    h_p, conv_p, pool_p, (w2u, w2d) = _mixer_prompt(
        h_p, mod_p, *mixer_w, time_block=min(tp, 1024), sub_block=min(tp, 256),
        casts=(w_ffn2_up[0], w_ffn2_down[0]))
    h_s, conv_s, pool_s = _mixer_sample(h_s, mod_s, *mixer_w, sconv, spool, batch_block=min(bs, 32))

    y_p, y_s, _ = _ffn(h_p, h_s, mod_p, mod_s, 2, g_ffn2, w2u, w2d, g_final2, **tiles)
    return (y_p, jnp.transpose(y_s, (1, 0, 2)), conv_p, pool_p,
            jnp.transpose(conv_s, (0, 2, 1, 3)), jnp.transpose(pool_s, (0, 2, 1, 3)))
```

```python
import functools

import jax
import jax.numpy as jnp
from jax import lax
from jax.experimental import pallas as pl
from jax.experimental.pallas import tpu as pltpu

EPS = 1e-6
HALF = 0.5
N_MOD = 9
CONV_WIDTH = 31
POOL_WINDOWS = (2, 4, 8, 16)
POOL_MAX = max(POOL_WINDOWS)
SAMPLE_POS0 = 16384

LANES = 128
BF16_SUBLANES = 16
FIRST_CHUNK_SLABS = 4
CONV_HALO = 32
POOL_HALO = 16
VMEM_LIMIT_BYTES = 56 * 1024 * 1024

F32 = jnp.float32
BF16 = jnp.bfloat16


def _dot(a, b):
    return jnp.dot(a, b, preferred_element_type=F32)


def _sigmoid(x):
    return 1.0 / (1.0 + jnp.exp(-x))


def _rms_scale(x):
    return lax.rsqrt(jnp.mean(x * x, axis=-1, keepdims=True) + EPS)


def _const_spec(shape):
    zeros = (0,) * len(shape)
    return pl.BlockSpec(shape, lambda *_: zeros, pipeline_mode=pl.Buffered(1))


def _cast_specs(casts, n_slabs, slab_of_step):
    specs = []
    for w in casts:
        rows, cols = w.shape
        assert rows % (n_slabs * BF16_SUBLANES) == 0, (w.shape, n_slabs)
        specs.append(pl.BlockSpec((rows // n_slabs, cols), lambda *idx: (slab_of_step(*idx), 0)))
    return specs


def _cast_slabs(cast_in, cast_out):
    for src, dst in zip(cast_in, cast_out):
        dst[...] = src[...].astype(BF16)


def _adaln_kernel(cp_ref, cs_ref, w_ref, b_ref, xs_ref, *rest, n_casts):
    cast_in, (op_ref, os_ref, xt_ref), cast_out = rest[:n_casts], rest[n_casts:n_casts + 3], rest[n_casts + 3:]
    w = w_ref[...].astype(BF16)
    b = b_ref[0]

    def mod(c):
        s = c * _sigmoid(c)
        return _dot(s.astype(BF16), w) + b

    op_ref[0] = mod(cp_ref[...])
    os_ref[0] = mod(cs_ref[...])
    _cast_slabs(cast_in, cast_out)
    for b in range(xs_ref.shape[0]):
        xt_ref[:, b, :] = xs_ref[b]


def _adaln(c_prompt, c_sample, w_ada, b_ada, x_sample, *, col_block, cast_slabs, casts=()):
    bp, d = c_prompt.shape
    bs, ts = x_sample.shape[0], x_sample.shape[1]
    seqs = bs // cast_slabs
    assert bs % (cast_slabs * 8) == 0
    per_mod = d // col_block
    n_steps = N_MOD * per_mod
    assert d % col_block == 0 and n_steps >= cast_slabs
    cast_specs = _cast_specs(casts, cast_slabs, lambda j: jnp.minimum(j, cast_slabs - 1))
    outs = pl.pallas_call(
        functools.partial(_adaln_kernel, n_casts=len(casts)),
        grid=(n_steps,),
        in_specs=[
            pl.BlockSpec((bp, d), lambda j: (0, 0)),
            pl.BlockSpec((bs, d), lambda j: (0, 0)),
            pl.BlockSpec((d, col_block), lambda j: (0, j)),
            pl.BlockSpec((1, 1, col_block), lambda j: (j, 0, 0)),
            pl.BlockSpec((seqs, ts, d), lambda j: (jnp.minimum(j, cast_slabs - 1), 0, 0)),
        ] + cast_specs,
        out_specs=[
            pl.BlockSpec((1, bp, col_block), lambda j: (j // per_mod, 0, j % per_mod)),
            pl.BlockSpec((1, bs, col_block), lambda j: (j // per_mod, 0, j % per_mod)),
            pl.BlockSpec((ts, seqs, d), lambda j: (0, jnp.minimum(j, cast_slabs - 1), 0)),
        ] + cast_specs,
        out_shape=[
            jax.ShapeDtypeStruct((N_MOD, bp, d), F32),
            jax.ShapeDtypeStruct((N_MOD, bs, d), F32),
            jax.ShapeDtypeStruct((ts, bs, d), F32),
        ] + [jax.ShapeDtypeStruct(w.shape, BF16) for w in casts],
        compiler_params=pltpu.CompilerParams(
            dimension_semantics=("arbitrary",), vmem_limit_bytes=VMEM_LIMIT_BYTES),
        name="adaln",
    )(c_prompt, c_sample, w_ada, b_ada.reshape(n_steps, 1, col_block), x_sample, *casts)
    return outs[0], outs[1], outs[2], outs[3:]


def _ffn_in(x, mod, gain, xb_scr, r0=0):
    shift, scale, _ = mod
    xn = (x * _rms_scale(x)) * gain * (1.0 + scale) + shift
    m = xn.size // xn.shape[-1]
    xb_scr[r0:r0 + m, :] = xn.reshape(m, xn.shape[-1]).astype(BF16)


def _ffn_chunk(lo, ff_chunk, wup_ref, xb_scr, a_scr, rows=slice(None)):
    d_ff = a_scr.shape[1]
    xb = xb_scr[rows, :]
    g = _dot(xb, wup_ref[:, lo:lo + ff_chunk])
    u = _dot(xb, wup_ref[:, d_ff + lo:d_ff + lo + ff_chunk])
    a_scr[rows, lo:lo + ff_chunk] = (g * _sigmoid(g) * u).astype(BF16)


def _ffn_out(x, mod, wdn_ref, a_scr, g_final, rows=slice(None)):
    gate = mod[2]
    out = x + (HALF * gate) * _dot(a_scr[rows, :], wdn_ref[...]).reshape(x.shape)
    if g_final is not None:
        out = (out * _rms_scale(out)) * g_final
    return out


def _ffn_block(x_ref, mod_ref, o_ref, gain, wup_ref, wdn_ref, xb_scr, a_scr, g_final, ff_chunk):
    mod = (mod_ref[0], mod_ref[1], mod_ref[2])
    lead, rows, _ = x_ref.shape
    used = slice(0, lead * rows)
    if lead == 1 and rows % (FIRST_CHUNK_SLABS * BF16_SUBLANES) == 0:
        slab = rows // FIRST_CHUNK_SLABS
        for r0 in range(0, rows, slab):
            _ffn_in(x_ref[:, r0:r0 + slab, :], mod, gain, xb_scr, r0)
            _ffn_chunk(0, ff_chunk, wup_ref, xb_scr, a_scr, slice(r0, r0 + slab))
        first = ff_chunk
    else:
        _ffn_in(x_ref[...], mod, gain, xb_scr)
        first = 0
    for lo in range(first, a_scr.shape[1], ff_chunk):
        _ffn_chunk(lo, ff_chunk, wup_ref, xb_scr, a_scr, used)
    out = _ffn_out(x_ref[...], mod, wdn_ref, a_scr, g_final, used)
    if o_ref.shape == x_ref.shape:
        o_ref[...] = out
    else:
        for t in range(lead):
            o_ref[:, t, :] = out[t]


def _ffn_kernel(xp_ref, modp_ref, xs_ref, mods_ref, g_ref, wup_ref, wdn_ref, *rest,
                ff_chunk, final_norm, n_casts, prompt_steps):
    rest = list(rest)
    gf_ref = rest.pop(0) if final_norm else None
    cast_in = [rest.pop(0) for _ in range(n_casts)]
    op_ref, os_ref = rest.pop(0), rest.pop(0)
    cast_out = [rest.pop(0) for _ in range(n_casts)]
    xb_scr, a_scr = rest
    step = pl.program_id(0)
    g_final = gf_ref[...] if final_norm else None

    @pl.when(step < prompt_steps)
    def _():
        _ffn_block(xp_ref, modp_ref, op_ref, g_ref[...], wup_ref, wdn_ref, xb_scr, a_scr, g_final, ff_chunk)

    @pl.when(step >= prompt_steps)
    def _():
        _ffn_block(xs_ref, mods_ref, os_ref, g_ref[...], wup_ref, wdn_ref, xb_scr, a_scr, g_final, ff_chunk)

    _cast_slabs(cast_in, cast_out)


def _ffn(xp, xs, mod_p, mod_s, mod_block, gain, w_up, w_down, g_final, *, tile_p, seq_s, ff_chunk=256, casts=(),
         out_batch_major=False):
    bp, tp, d = xp.shape
    ts, bs, _ = xs.shape
    d_ff = w_down.shape[0]
    tiles = tp // tile_p
    prompt_steps = bp * tiles
    sample_steps = bs // seq_s
    final_norm = g_final is not None

    def p_tile(s):
        s = jnp.minimum(s, prompt_steps - 1)
        return s // tiles, s % tiles

    def s_block(s):
        return jnp.maximum(s - prompt_steps, 0)

    xp_spec = pl.BlockSpec((1, tile_p, d), lambda s: (*p_tile(s), 0))
    tm_spec = pl.BlockSpec((ts, seq_s, d), lambda s: (0, s_block(s), 0))
    bm_spec = pl.BlockSpec((seq_s, ts, d), lambda s: (s_block(s), 0, 0))
    in_specs = [
        xp_spec,
        pl.BlockSpec((3, 1, 1, d), lambda s: (mod_block, p_tile(s)[0], 0, 0)),
        tm_spec,
        pl.BlockSpec((3, 1, seq_s, d), lambda s: (mod_block, 0, s_block(s), 0)),
        _const_spec((1, d)), _const_spec((d, 2 * d_ff)), _const_spec((d_ff, d)),
    ]
    args = [xp, mod_p, xs, mod_s, gain, w_up, w_down]
    if final_norm:
        in_specs.append(_const_spec((1, d)))
        args.append(g_final)
    cast_specs = _cast_specs(casts, prompt_steps, lambda s: jnp.minimum(s, prompt_steps - 1))
    m = max(tile_p, ts * seq_s)
    outs = pl.pallas_call(
        functools.partial(_ffn_kernel, ff_chunk=ff_chunk, final_norm=final_norm, n_casts=len(casts),
                          prompt_steps=prompt_steps),
        grid=(prompt_steps + sample_steps,),
        in_specs=in_specs + cast_specs,
        out_specs=[xp_spec, bm_spec if out_batch_major else tm_spec] + cast_specs,
        out_shape=[jax.ShapeDtypeStruct(xp.shape, F32),
                   jax.ShapeDtypeStruct((bs, ts, d) if out_batch_major else (ts, bs, d), F32)]
        + [jax.ShapeDtypeStruct(w.shape, BF16) for w in casts],
        scratch_shapes=[pltpu.VMEM((m, d), BF16), pltpu.VMEM((m, d_ff), BF16)],
        compiler_params=pltpu.CompilerParams(
            dimension_semantics=("arbitrary",), vmem_limit_bytes=VMEM_LIMIT_BYTES),
        name="ffn_final" if final_norm else "ffn",
    )(*args, *casts)
    return outs[0], outs[1], outs[2:]


def _layernorm_silu(v, ln_g, ln_b):
    mu = jnp.mean(v, axis=-1, keepdims=True)
    c = v - mu
    var = jnp.mean(c * c, axis=-1, keepdims=True)
    y = (c * lax.rsqrt(var + EPS)) * ln_g + ln_b
    return y * _sigmoid(y)


def _mixer_rows_in(x, mod, gain, win_ref, cs, ps, r0):
    sb = x.shape[0]
    n_slabs = cs.shape[0]
    d_conv = n_slabs * LANES
    shift, scale, _ = mod
    hm = (x * _rms_scale(x)) * gain * (1.0 + scale) + shift
    proj = _dot(hm.astype(BF16), win_ref[...])
    u_conv = proj[:, :d_conv] * _sigmoid(proj[:, d_conv:2 * d_conv])
    for s in range(n_slabs):
        cs[s, CONV_HALO + r0:CONV_HALO + r0 + sb, :] = u_conv[:, s * LANES:(s + 1) * LANES]
        ps[s, POOL_HALO + r0:POOL_HALO + r0 + sb, :] = (
            proj[:, 2 * d_conv + s * LANES:2 * d_conv + (s + 1) * LANES])


def _mixer_rows_out(x, mod, pos0, r0, row_block, wdw_ref, bdw_ref, lng_ref, lnb_ref,
                    wpool_ref, pscale_ref, wout_ref, cs, ps, v_scr, mix_scr):
    sb = x.shape[0]
    n_slabs = cs.shape[0]
    d_conv = n_slabs * LANES
    assert len(POOL_WINDOWS) == ps.shape[0] and wpool_ref.shape[1] == LANES
    gate = mod[2]
    rows = slice(r0, r0 + sb)

    first = CONV_HALO - (CONV_WIDTH - 1)
    for s in range(n_slabs):
        lanes = slice(s * LANES, (s + 1) * LANES)
        for r in range(r0, r0 + sb, row_block):
            acc = jnp.broadcast_to(bdw_ref[:, lanes], (row_block, LANES))
            for k in range(CONV_WIDTH):
                acc = acc + wdw_ref[k:k + 1, lanes] * cs[s, first + k + r:first + k + r + row_block, :]
            v_scr[r:r + row_block, lanes] = acc
    mix_scr[rows, 0:d_conv] = _layernorm_silu(v_scr[rows, :], lng_ref[...], lnb_ref[...]).astype(BF16)

    pos = pos0 + r0 + lax.broadcasted_iota(jnp.int32, (sb, 1), 0)
    for gi, w in enumerate(POOL_WINDOWS):
        lanes = slice(gi * LANES, (gi + 1) * LANES)
        cur = ps[gi, POOL_HALO + r0:POOL_HALO + r0 + sb, :]
        win_sum = cur
        for j in range(1, w):
            win_sum = win_sum + ps[gi, POOL_HALO + r0 - j:POOL_HALO + r0 - j + sb, :]
        count = jnp.minimum(pos + 1, w).astype(F32)
        pooled = win_sum / count - cur
        mixed = _dot(pooled.astype(BF16), wpool_ref[gi])
        mix_scr[rows, d_conv + gi * LANES:d_conv + (gi + 1) * LANES] = (
            mixed * pscale_ref[:, lanes]).astype(BF16)

    return x + gate * _dot(mix_scr[rows, :], wout_ref[...])


def _mixer_prompt_kernel(x_ref, mod_ref, g_ref, win_ref, wdw_ref, bdw_ref, lng_ref, lnb_ref,
                         wpool_ref, pscale_ref, wout_ref, *rest, sub_block, row_block, n_casts):
    cast_in, (o_ref, oconv_ref, opool_ref) = rest[:n_casts], rest[n_casts:n_casts + 3]
    cast_out, (cs, ps, v_scr, mix_scr) = rest[n_casts + 3:2 * n_casts + 3], rest[2 * n_casts + 3:]
    t_idx = pl.program_id(1)
    n_t = pl.num_programs(1)
    tm = x_ref.shape[1]
    n_slabs = cs.shape[0]

    @pl.when(t_idx == 0)
    def _():
        cs[:, 0:CONV_HALO, :] = jnp.zeros((n_slabs, CONV_HALO, LANES), F32)
        ps[:, 0:POOL_HALO, :] = jnp.zeros((ps.shape[0], POOL_HALO, LANES), F32)

    mod = (mod_ref[0, 0], mod_ref[1, 0], mod_ref[2, 0])
    _mixer_rows_in(x_ref[0], mod, g_ref[...], win_ref, cs, ps, 0)
    for r0 in range(0, tm, sub_block):
        x = x_ref[0, r0:r0 + sub_block, :]
        o_ref[0, r0:r0 + sub_block, :] = _mixer_rows_out(
            x, mod, t_idx * tm, r0, row_block, wdw_ref, bdw_ref,
            lng_ref, lnb_ref, wpool_ref, pscale_ref, wout_ref, cs, ps, v_scr, mix_scr)

    @pl.when(t_idx == n_t - 1)
    def _():
        for s in range(n_slabs):
            lanes = slice(s * LANES, (s + 1) * LANES)
            oconv_ref[0, 0, :, lanes] = cs[s, CONV_HALO + tm - (CONV_WIDTH - 1):CONV_HALO + tm, :]
            opool_ref[0, 0, :, lanes] = ps[s, POOL_HALO + tm - (POOL_MAX - 1):POOL_HALO + tm, :]

    cs[:, 0:CONV_HALO, :] = cs[:, tm:tm + CONV_HALO, :]
    ps[:, 0:POOL_HALO, :] = ps[:, tm:tm + POOL_HALO, :]
    _cast_slabs(cast_in, cast_out)


def _mixer_prompt(x, mod4, gain, w_in, w_dw, b_dw, ln_g, ln_b, w_pool, pool_scale, w_out,
                  *, time_block, sub_block, row_block=16, casts=()):
    b, t, d = x.shape
    d_conv = w_dw.shape[1]
    d_pool = pool_scale.shape[1]
    n_groups, group = w_pool.shape[0], w_pool.shape[1]
    assert group == LANES and d_conv % LANES == 0 and d_pool == n_groups * LANES
    assert time_block % sub_block == 0 and sub_block % row_block == 0
    x_spec = pl.BlockSpec((1, time_block, d), lambda i, j: (i, j, 0))
    tiles = t // time_block
    cast_specs = _cast_specs(casts, b * tiles, lambda i, j: i * tiles + j)
    outs = pl.pallas_call(
        functools.partial(_mixer_prompt_kernel, sub_block=sub_block, row_block=row_block, n_casts=len(casts)),
        grid=(b, tiles),
        in_specs=[
            x_spec,
            pl.BlockSpec((3, 1, 1, d), lambda i, j: (1, i, 0, 0)),
            _const_spec((1, d)),
            _const_spec((d, 2 * d_conv + d_pool)),
            _const_spec((CONV_WIDTH, d_conv)),
            _const_spec((1, d_conv)),
            _const_spec((1, d_conv)),
            _const_spec((1, d_conv)),
            _const_spec((n_groups, group, group)),
            _const_spec((1, d_pool)),
            _const_spec((d_conv + d_pool, d)),
        ] + cast_specs,
        out_specs=[
            x_spec,
            pl.BlockSpec((1, 1, CONV_WIDTH - 1, d_conv), lambda i, j: (0, i, 0, 0)),
            pl.BlockSpec((1, 1, POOL_MAX - 1, d_pool), lambda i, j: (0, i, 0, 0)),
        ] + cast_specs,
        out_shape=[
            jax.ShapeDtypeStruct(x.shape, F32),
            jax.ShapeDtypeStruct((1, b, CONV_WIDTH - 1, d_conv), F32),
            jax.ShapeDtypeStruct((1, b, POOL_MAX - 1, d_pool), F32),
        ] + [jax.ShapeDtypeStruct(w.shape, BF16) for w in casts],
        scratch_shapes=[
            pltpu.VMEM((d_conv // LANES, CONV_HALO + time_block, LANES), F32),
            pltpu.VMEM((n_groups, POOL_HALO + time_block, LANES), F32),
            pltpu.VMEM((time_block, d_conv), F32),
            pltpu.VMEM((time_block, d_conv + d_pool), BF16),
        ],
        compiler_params=pltpu.CompilerParams(
            dimension_semantics=("arbitrary", "arbitrary"), vmem_limit_bytes=VMEM_LIMIT_BYTES),
        name="mixer_prompt",
    )(x, mod4, gain, w_in, w_dw, b_dw, ln_g, ln_b, w_pool, pool_scale, w_out, *casts)
    return outs[0], outs[1], outs[2], outs[3:]


def _mixer_sample_kernel(x_ref, mod_ref, g_ref, win_ref, wdw_ref, bdw_ref, lng_ref, lnb_ref,
                         wpool_ref, pscale_ref, wout_ref, sconv_ref, spool_ref,
                         o_ref, oconv_ref, opool_ref, mix_scr):
    tt, bb, d = x_ref.shape
    m = tt * bb
    n_conv, d_conv = sconv_ref.shape[1], sconv_ref.shape[3]
    n_pool = spool_ref.shape[1]

    x = x_ref[...]
    shift, scale, gate = mod_ref[0], mod_ref[1], mod_ref[2]
    hm = (x * _rms_scale(x)) * g_ref[...] * (1.0 + scale) + shift
    proj = _dot(hm.reshape(m, d).astype(BF16), win_ref[...])
    u_conv = proj[:, :d_conv] * _sigmoid(proj[:, d_conv:2 * d_conv])
    u_pool = proj[:, 2 * d_conv:]

    def conv_in(s):
        return sconv_ref[0, s] if s < n_conv else u_conv[(s - n_conv) * bb:(s - n_conv + 1) * bb, :]

    def pool_in(s, lanes=slice(None)):
        return spool_ref[0, s, :, lanes] if s < n_pool else u_pool[(s - n_pool) * bb:(s - n_pool + 1) * bb, lanes]

    for s in range(n_conv):
        oconv_ref[0, s] = conv_in(s + tt)
    for s in range(n_pool):
        opool_ref[0, s] = pool_in(s + tt)

    for t in range(tt):
        acc = jnp.broadcast_to(bdw_ref[...], (bb, d_conv))
        for k in range(CONV_WIDTH):
            acc = acc + wdw_ref[k:k + 1, :] * conv_in(t + k)
        mix_scr[t * bb:(t + 1) * bb, 0:d_conv] = _layernorm_silu(acc, lng_ref[...], lnb_ref[...]).astype(BF16)

    for gi, w in enumerate(POOL_WINDOWS):
        lanes = slice(gi * LANES, (gi + 1) * LANES)
        count = float(min(SAMPLE_POS0 + 1, w))
        for t in range(tt):
            cur = pool_in(n_pool + t, lanes)
            win_sum = cur
            for j in range(1, w):
                win_sum = win_sum + pool_in(n_pool + t - j, lanes)
            pooled = win_sum / count - cur
            mixed = _dot(pooled.astype(BF16), wpool_ref[gi])
            mix_scr[t * bb:(t + 1) * bb, d_conv + gi * LANES:d_conv + (gi + 1) * LANES] = (
                mixed * pscale_ref[:, lanes]).astype(BF16)

    o_ref[...] = x + gate * _dot(mix_scr[...], wout_ref[...]).reshape(tt, bb, d)


def _mixer_sample(x, mod4, gain, w_in, w_dw, b_dw, ln_g, ln_b, w_pool, pool_scale, w_out,
                  state_conv, state_pool, *, batch_block):
    t, b, d = x.shape
    d_conv = w_dw.shape[1]
    d_pool = pool_scale.shape[1]
    n_groups, group = w_pool.shape[0], w_pool.shape[1]
    assert group == LANES and d_pool == n_groups * LANES and d_conv == d_pool
    x_spec = pl.BlockSpec((t, batch_block, d), lambda i: (0, i, 0))
    conv_spec = pl.BlockSpec((1, CONV_WIDTH - 1, batch_block, d_conv), lambda i: (0, 0, i, 0))
    pool_spec = pl.BlockSpec((1, POOL_MAX - 1, batch_block, d_pool), lambda i: (0, 0, i, 0))
    return pl.pallas_call(
        _mixer_sample_kernel,
        grid=(b // batch_block,),
        in_specs=[
            x_spec,
            pl.BlockSpec((3, 1, batch_block, d), lambda i: (1, 0, i, 0)),
            _const_spec((1, d)),
            _const_spec((d, 2 * d_conv + d_pool)),
            _const_spec((CONV_WIDTH, d_conv)),
            _const_spec((1, d_conv)),
            _const_spec((1, d_conv)),
            _const_spec((1, d_conv)),
            _const_spec((n_groups, group, group)),
            _const_spec((1, d_pool)),
            _const_spec((d_conv + d_pool, d)),
            conv_spec,
            pool_spec,
        ],
        out_specs=[x_spec, conv_spec, pool_spec],
        out_shape=[
            jax.ShapeDtypeStruct(x.shape, F32),
            jax.ShapeDtypeStruct(state_conv.shape, F32),
            jax.ShapeDtypeStruct(state_pool.shape, F32),
        ],
        scratch_shapes=[pltpu.VMEM((t * batch_block, d_conv + d_pool), BF16)],
        compiler_params=pltpu.CompilerParams(
            dimension_semantics=("arbitrary",), vmem_limit_bytes=VMEM_LIMIT_BYTES),
        name="mixer_sample",
    )(x, mod4, gain, w_in, w_dw, b_dw, ln_g, ln_b, w_pool, pool_scale, w_out, state_conv, state_pool)


def kernel(x_prompt, x_sample, state_conv, state_pool, c_prompt, c_sample, w_ada, b_ada, g_ffn1, w_ffn1_up, w_ffn1_down, g_mix, w_in, w_dw, b_dw, ln_g, ln_b, w_pool, pool_scale, w_out, g_ffn2, w_ffn2_up, w_ffn2_down, g_final):
    depth = w_ada.shape[0]
    assert depth == 1 and state_conv.shape[0] == 1 and state_pool.shape[0] == 1
    bp, tp, d = x_prompt.shape
    bs, ts, _ = x_sample.shape

    mod_p, mod_s, xs, (w1u, w1d) = _adaln(c_prompt, c_sample, w_ada[0], b_ada[0], x_sample, col_block=d,
                                          cast_slabs=8, casts=(w_ffn1_up[0], w_ffn1_down[0]))
    mod_p = mod_p.reshape(N_MOD, bp, 1, d)
    mod_s = mod_s.reshape(N_MOD, 1, bs, d)

    g_final2 = g_final.reshape(1, d)
    n_groups, group, _ = w_pool.shape[1:]

    tiles = dict(tile_p=min(tp, 1024), seq_s=min(bs, 64))
    sconv = jnp.transpose(state_conv, (0, 2, 1, 3))
    spool = jnp.transpose(state_pool, (0, 2, 1, 3))

    later = (w_in[0], w_out[0], w_pool[0].reshape(n_groups * group, group))
    h_p, h_s, (w_in_b, w_out_b, w_pool_b) = _ffn(
        x_prompt, xs, mod_p, mod_s, 0, g_ffn1, w1u, w1d, None, casts=later, **tiles)
    mixer_w = (g_mix, w_in_b, w_dw[0], b_dw, ln_g, ln_b, w_pool_b.reshape(n_groups, group, group),
               pool_scale, w_out_b)

    h_p, conv_p, pool_p, (w2u, w2d) = _mixer_prompt(
        h_p, mod_p, *mixer_w, time_block=min(tp, 1024), sub_block=min(tp, 256),
        casts=(w_ffn2_up[0], w_ffn2_down[0]))
    h_s, conv_s, pool_s = _mixer_sample(h_s, mod_s, *mixer_w, sconv, spool, batch_block=min(bs, 32))

    y_p, y_s, _ = _ffn(h_p, h_s, mod_p, mod_s, 2, g_ffn2, w2u, w2d, g_final2, out_batch_major=True, **tiles)
    return (y_p, y_s, conv_p, pool_p,
            jnp.transpose(conv_s, (0, 2, 1, 3)), jnp.transpose(pool_s, (0, 2, 1, 3)))
```

```python
import functools

import jax
import jax.numpy as jnp
from jax import lax
from jax.experimental import pallas as pl
from jax.experimental.pallas import tpu as pltpu

EPS = 1e-6
HALF = 0.5
N_MOD = 9
CONV_WIDTH = 31
POOL_WINDOWS = (2, 4, 8, 16)
POOL_MAX = max(POOL_WINDOWS)
SAMPLE_POS0 = 16384

LANES = 128
BF16_SUBLANES = 16
FIRST_CHUNK_SLABS = 4
CONV_HALO = 32
POOL_HALO = 16
VMEM_LIMIT_BYTES = 56 * 1024 * 1024

F32 = jnp.float32
BF16 = jnp.bfloat16


def _dot(a, b):
    return jnp.dot(a, b, preferred_element_type=F32)


def _sigmoid(x):
    return 1.0 / (1.0 + jnp.exp(-x))


def _rms_scale(x):
    return lax.rsqrt(jnp.mean(x * x, axis=-1, keepdims=True) + EPS)


def _const_spec(shape):
    zeros = (0,) * len(shape)
    return pl.BlockSpec(shape, lambda *_: zeros, pipeline_mode=pl.Buffered(1))


def _cast_specs(casts, n_slabs, slab_of_step):
    specs = []
    for w in casts:
        rows, cols = w.shape
        assert rows % (n_slabs * BF16_SUBLANES) == 0, (w.shape, n_slabs)
        specs.append(pl.BlockSpec((rows // n_slabs, cols), lambda *idx: (slab_of_step(*idx), 0)))
    return specs


def _cast_slabs(cast_in, cast_out):
    for src, dst in zip(cast_in, cast_out):
        dst[...] = src[...].astype(BF16)


def _adaln_kernel(cp_ref, cs_ref, w_ref, b_ref, *rest, n_casts):
    cast_in, (op_ref, os_ref), cast_out = rest[:n_casts], rest[n_casts:n_casts + 2], rest[n_casts + 2:]
    w = w_ref[...].astype(BF16)
    b = b_ref[0]

    def mod(c):
        s = c * _sigmoid(c)
        return _dot(s.astype(BF16), w) + b

    op_ref[0] = mod(cp_ref[...])
    os_ref[0] = mod(cs_ref[...])
    _cast_slabs(cast_in, cast_out)


def _adaln(c_prompt, c_sample, w_ada, b_ada, *, col_block, cast_slabs, casts=()):
    bp, d = c_prompt.shape
    bs = c_sample.shape[0]
    per_mod = d // col_block
    n_steps = N_MOD * per_mod
    assert d % col_block == 0 and n_steps >= cast_slabs
    cast_specs = _cast_specs(casts, cast_slabs, lambda j: jnp.minimum(j, cast_slabs - 1))
    outs = pl.pallas_call(
        functools.partial(_adaln_kernel, n_casts=len(casts)),
        grid=(n_steps,),
        in_specs=[
            pl.BlockSpec((bp, d), lambda j: (0, 0)),
            pl.BlockSpec((bs, d), lambda j: (0, 0)),
            pl.BlockSpec((d, col_block), lambda j: (0, j)),
            pl.BlockSpec((1, 1, col_block), lambda j: (j, 0, 0)),
        ] + cast_specs,
        out_specs=[
            pl.BlockSpec((1, bp, col_block), lambda j: (j // per_mod, 0, j % per_mod)),
            pl.BlockSpec((1, bs, col_block), lambda j: (j // per_mod, 0, j % per_mod)),
        ] + cast_specs,
        out_shape=[
            jax.ShapeDtypeStruct((N_MOD, bp, d), F32),
            jax.ShapeDtypeStruct((N_MOD, bs, d), F32),
        ] + [jax.ShapeDtypeStruct(w.shape, BF16) for w in casts],
        compiler_params=pltpu.CompilerParams(
            dimension_semantics=("arbitrary",), vmem_limit_bytes=VMEM_LIMIT_BYTES),
        name="adaln",
    )(c_prompt, c_sample, w_ada, b_ada.reshape(n_steps, 1, col_block), *casts)
    return outs[0], outs[1], outs[2:]


def _ffn_in(x, mod, gain, xb_scr, r0=0):
    shift, scale, _ = mod
    xn = (x * _rms_scale(x)) * gain * (1.0 + scale) + shift
    m = xn.size // xn.shape[-1]
    xb_scr[r0:r0 + m, :] = xn.reshape(m, xn.shape[-1]).astype(BF16)


def _ffn_chunk(lo, ff_chunk, wup_ref, xb_scr, a_scr, rows=slice(None)):
    d_ff = a_scr.shape[1]
    xb = xb_scr[rows, :]
    g = _dot(xb, wup_ref[:, lo:lo + ff_chunk])
    u = _dot(xb, wup_ref[:, d_ff + lo:d_ff + lo + ff_chunk])
    a_scr[rows, lo:lo + ff_chunk] = (g * _sigmoid(g) * u).astype(BF16)


def _ffn_out(x, mod, wdn_ref, a_scr, g_final, rows=slice(None)):
    gate = mod[2]
    out = x + (HALF * gate) * _dot(a_scr[rows, :], wdn_ref[...]).reshape(x.shape)
    if g_final is not None:
        out = (out * _rms_scale(out)) * g_final
    return out


def _ffn_block(x_ref, mod_ref, o_ref, gain, wup_ref, wdn_ref, xb_scr, a_scr, g_final, ff_chunk):
    mod = (mod_ref[0], mod_ref[1], mod_ref[2])
    lead, rows, _ = x_ref.shape
    used = slice(0, lead * rows)
    if lead == 1 and rows % (FIRST_CHUNK_SLABS * BF16_SUBLANES) == 0:
        slab = rows // FIRST_CHUNK_SLABS
        for r0 in range(0, rows, slab):
            _ffn_in(x_ref[:, r0:r0 + slab, :], mod, gain, xb_scr, r0)
            _ffn_chunk(0, ff_chunk, wup_ref, xb_scr, a_scr, slice(r0, r0 + slab))
        first = ff_chunk
    else:
        _ffn_in(x_ref[...], mod, gain, xb_scr)
        first = 0
    for lo in range(first, a_scr.shape[1], ff_chunk):
        _ffn_chunk(lo, ff_chunk, wup_ref, xb_scr, a_scr, used)
    out = _ffn_out(x_ref[...], mod, wdn_ref, a_scr, g_final, used)
    if o_ref.shape == x_ref.shape:
        o_ref[...] = out
    else:
        for t in range(lead):
            o_ref[:, t, :] = out[t]


def _ffn_kernel(xp_ref, modp_ref, xs_ref, mods_ref, g_ref, wup_ref, wdn_ref, *rest,
                ff_chunk, final_norm, n_casts, prompt_steps):
    rest = list(rest)
    gf_ref = rest.pop(0) if final_norm else None
    cast_in = [rest.pop(0) for _ in range(n_casts)]
    op_ref, os_ref = rest.pop(0), rest.pop(0)
    cast_out = [rest.pop(0) for _ in range(n_casts)]
    xb_scr, a_scr = rest
    step = pl.program_id(0)
    g_final = gf_ref[...] if final_norm else None

    @pl.when(step < prompt_steps)
    def _():
        _ffn_block(xp_ref, modp_ref, op_ref, g_ref[...], wup_ref, wdn_ref, xb_scr, a_scr, g_final, ff_chunk)

    @pl.when(step >= prompt_steps)
    def _():
        _ffn_block(xs_ref, mods_ref, os_ref, g_ref[...], wup_ref, wdn_ref, xb_scr, a_scr, g_final, ff_chunk)

    _cast_slabs(cast_in, cast_out)


def _ffn(xp, xs, mod_p, mod_s, mod_block, gain, w_up, w_down, g_final, *, tile_p, seq_s, ff_chunk=256, casts=(),
         out_batch_major=False, prompt_in_place=False):
    bp, tp, d = xp.shape
    ts, bs, _ = xs.shape
    d_ff = w_down.shape[0]
    tiles = tp // tile_p
    prompt_steps = bp * tiles
    sample_steps = bs // seq_s
    final_norm = g_final is not None

    def p_tile(s):
        s = jnp.minimum(s, prompt_steps - 1)
        return s // tiles, s % tiles

    def s_block(s):
        return jnp.maximum(s - prompt_steps, 0)

    xp_spec = pl.BlockSpec((1, tile_p, d), lambda s: (*p_tile(s), 0))
    tm_spec = pl.BlockSpec((ts, seq_s, d), lambda s: (0, s_block(s), 0))
    bm_spec = pl.BlockSpec((seq_s, ts, d), lambda s: (s_block(s), 0, 0))
    in_specs = [
        xp_spec,
        pl.BlockSpec((3, 1, 1, d), lambda s: (mod_block, p_tile(s)[0], 0, 0)),
        tm_spec,
        pl.BlockSpec((3, 1, seq_s, d), lambda s: (mod_block, 0, s_block(s), 0)),
        _const_spec((1, d)), _const_spec((d, 2 * d_ff)), _const_spec((d_ff, d)),
    ]
    args = [xp, mod_p, xs, mod_s, gain, w_up, w_down]
    if final_norm:
        in_specs.append(_const_spec((1, d)))
        args.append(g_final)
    cast_specs = _cast_specs(casts, prompt_steps, lambda s: jnp.minimum(s, prompt_steps - 1))
    m = max(tile_p, ts * seq_s)
    outs = pl.pallas_call(
        functools.partial(_ffn_kernel, ff_chunk=ff_chunk, final_norm=final_norm, n_casts=len(casts),
                          prompt_steps=prompt_steps),
        grid=(prompt_steps + sample_steps,),
        in_specs=in_specs + cast_specs,
        out_specs=[xp_spec, bm_spec if out_batch_major else tm_spec] + cast_specs,
        out_shape=[jax.ShapeDtypeStruct(xp.shape, F32),
                   jax.ShapeDtypeStruct((bs, ts, d) if out_batch_major else (ts, bs, d), F32)]
        + [jax.ShapeDtypeStruct(w.shape, BF16) for w in casts],
        scratch_shapes=[pltpu.VMEM((m, d), BF16), pltpu.VMEM((m, d_ff), BF16)],
        input_output_aliases={0: 0} if prompt_in_place else {},
        compiler_params=pltpu.CompilerParams(
            dimension_semantics=("arbitrary",), vmem_limit_bytes=VMEM_LIMIT_BYTES),
        name="ffn_final" if final_norm else "ffn",
    )(*args, *casts)
    return outs[0], outs[1], outs[2:]


def _layernorm_silu(v, ln_g, ln_b):
    mu = jnp.mean(v, axis=-1, keepdims=True)
    c = v - mu
    var = jnp.mean(c * c, axis=-1, keepdims=True)
    y = (c * lax.rsqrt(var + EPS)) * ln_g + ln_b
    return y * _sigmoid(y)


def _mixer_rows_in(x, mod, gain, win_ref, cs, ps, r0):
    sb = x.shape[0]
    n_slabs = cs.shape[0]
    d_conv = n_slabs * LANES
    shift, scale, _ = mod
    hm = (x * _rms_scale(x)) * gain * (1.0 + scale) + shift
    proj = _dot(hm.astype(BF16), win_ref[...])
    u_conv = proj[:, :d_conv] * _sigmoid(proj[:, d_conv:2 * d_conv])
    for s in range(n_slabs):
        cs[s, CONV_HALO + r0:CONV_HALO + r0 + sb, :] = u_conv[:, s * LANES:(s + 1) * LANES]
        ps[s, POOL_HALO + r0:POOL_HALO + r0 + sb, :] = (
            proj[:, 2 * d_conv + s * LANES:2 * d_conv + (s + 1) * LANES])


def _mixer_rows_out(x, mod, pos0, r0, row_block, wdw_ref, bdw_ref, lng_ref, lnb_ref,
                    wpool_ref, pscale_ref, wout_ref, cs, ps, v_scr, mix_scr):
    sb = x.shape[0]
    n_slabs = cs.shape[0]
    d_conv = n_slabs * LANES
    assert len(POOL_WINDOWS) == ps.shape[0] and wpool_ref.shape[1] == LANES
    gate = mod[2]
    rows = slice(r0, r0 + sb)

    first = CONV_HALO - (CONV_WIDTH - 1)
    for s in range(n_slabs):
        lanes = slice(s * LANES, (s + 1) * LANES)
        for r in range(r0, r0 + sb, row_block):
            acc = jnp.broadcast_to(bdw_ref[:, lanes], (row_block, LANES))
            for k in range(CONV_WIDTH):
                acc = acc + wdw_ref[k:k + 1, lanes] * cs[s, first + k + r:first + k + r + row_block, :]
            v_scr[r:r + row_block, lanes] = acc
    mix_scr[rows, 0:d_conv] = _layernorm_silu(v_scr[rows, :], lng_ref[...], lnb_ref[...]).astype(BF16)

    pos = pos0 + r0 + lax.broadcasted_iota(jnp.int32, (sb, 1), 0)
    for gi, w in enumerate(POOL_WINDOWS):
        lanes = slice(gi * LANES, (gi + 1) * LANES)
        cur = ps[gi, POOL_HALO + r0:POOL_HALO + r0 + sb, :]
        win_sum = cur
        for j in range(1, w):
            win_sum = win_sum + ps[gi, POOL_HALO + r0 - j:POOL_HALO + r0 - j + sb, :]
        count = jnp.minimum(pos + 1, w).astype(F32)
        pooled = win_sum / count - cur
        mixed = _dot(pooled.astype(BF16), wpool_ref[gi])
        mix_scr[rows, d_conv + gi * LANES:d_conv + (gi + 1) * LANES] = (
            mixed * pscale_ref[:, lanes]).astype(BF16)

    return x + gate * _dot(mix_scr[rows, :], wout_ref[...])


def _mixer_prompt_kernel(x_ref, mod_ref, g_ref, win_ref, wdw_ref, bdw_ref, lng_ref, lnb_ref,
                         wpool_ref, pscale_ref, wout_ref, *rest, sub_block, row_block, n_casts):
    cast_in, (o_ref, oconv_ref, opool_ref) = rest[:n_casts], rest[n_casts:n_casts + 3]
    cast_out, (cs, ps, v_scr, mix_scr) = rest[n_casts + 3:2 * n_casts + 3], rest[2 * n_casts + 3:]
    t_idx = pl.program_id(1)
    n_t = pl.num_programs(1)
    tm = x_ref.shape[1]
    n_slabs = cs.shape[0]

    @pl.when(t_idx == 0)
    def _():
        cs[:, 0:CONV_HALO, :] = jnp.zeros((n_slabs, CONV_HALO, LANES), F32)
        ps[:, 0:POOL_HALO, :] = jnp.zeros((ps.shape[0], POOL_HALO, LANES), F32)

    mod = (mod_ref[0, 0], mod_ref[1, 0], mod_ref[2, 0])
    _mixer_rows_in(x_ref[0], mod, g_ref[...], win_ref, cs, ps, 0)
    for r0 in range(0, tm, sub_block):
        x = x_ref[0, r0:r0 + sub_block, :]
        o_ref[0, r0:r0 + sub_block, :] = _mixer_rows_out(
            x, mod, t_idx * tm, r0, row_block, wdw_ref, bdw_ref,
            lng_ref, lnb_ref, wpool_ref, pscale_ref, wout_ref, cs, ps, v_scr, mix_scr)

    @pl.when(t_idx == n_t - 1)
    def _():
        for s in range(n_slabs):
            lanes = slice(s * LANES, (s + 1) * LANES)
            oconv_ref[0, 0, :, lanes] = cs[s, CONV_HALO + tm - (CONV_WIDTH - 1):CONV_HALO + tm, :]
            opool_ref[0, 0, :, lanes] = ps[s, POOL_HALO + tm - (POOL_MAX - 1):POOL_HALO + tm, :]

    cs[:, 0:CONV_HALO, :] = cs[:, tm:tm + CONV_HALO, :]
    ps[:, 0:POOL_HALO, :] = ps[:, tm:tm + POOL_HALO, :]
    _cast_slabs(cast_in, cast_out)


def _mixer_prompt(x, mod4, gain, w_in, w_dw, b_dw, ln_g, ln_b, w_pool, pool_scale, w_out,
                  *, time_block, sub_block, row_block=16, casts=()):
    b, t, d = x.shape
    d_conv = w_dw.shape[1]
    d_pool = pool_scale.shape[1]
    n_groups, group = w_pool.shape[0], w_pool.shape[1]
    assert group == LANES and d_conv % LANES == 0 and d_pool == n_groups * LANES
    assert time_block % sub_block == 0 and sub_block % row_block == 0
    x_spec = pl.BlockSpec((1, time_block, d), lambda i, j: (i, j, 0))
    tiles = t // time_block
    cast_specs = _cast_specs(casts, b * tiles, lambda i, j: i * tiles + j)
    outs = pl.pallas_call(
        functools.partial(_mixer_prompt_kernel, sub_block=sub_block, row_block=row_block, n_casts=len(casts)),
        grid=(b, tiles),
        in_specs=[
            x_spec,
            pl.BlockSpec((3, 1, 1, d), lambda i, j: (1, i, 0, 0)),
            _const_spec((1, d)),
            _const_spec((d, 2 * d_conv + d_pool)),
            _const_spec((CONV_WIDTH, d_conv)),
            _const_spec((1, d_conv)),
            _const_spec((1, d_conv)),
            _const_spec((1, d_conv)),
            _const_spec((n_groups, group, group)),
            _const_spec((1, d_pool)),
            _const_spec((d_conv + d_pool, d)),
        ] + cast_specs,
        out_specs=[
            x_spec,
            pl.BlockSpec((1, 1, CONV_WIDTH - 1, d_conv), lambda i, j: (0, i, 0, 0)),
            pl.BlockSpec((1, 1, POOL_MAX - 1, d_pool), lambda i, j: (0, i, 0, 0)),
        ] + cast_specs,
        out_shape=[
            jax.ShapeDtypeStruct(x.shape, F32),
            jax.ShapeDtypeStruct((1, b, CONV_WIDTH - 1, d_conv), F32),
            jax.ShapeDtypeStruct((1, b, POOL_MAX - 1, d_pool), F32),
        ] + [jax.ShapeDtypeStruct(w.shape, BF16) for w in casts],
        scratch_shapes=[
            pltpu.VMEM((d_conv // LANES, CONV_HALO + time_block, LANES), F32),
            pltpu.VMEM((n_groups, POOL_HALO + time_block, LANES), F32),
            pltpu.VMEM((time_block, d_conv), F32),
            pltpu.VMEM((time_block, d_conv + d_pool), BF16),
        ],
        input_output_aliases={0: 0},
        compiler_params=pltpu.CompilerParams(
            dimension_semantics=("arbitrary", "arbitrary"), vmem_limit_bytes=VMEM_LIMIT_BYTES),
        name="mixer_prompt",
    )(x, mod4, gain, w_in, w_dw, b_dw, ln_g, ln_b, w_pool, pool_scale, w_out, *casts)
    return outs[0], outs[1], outs[2], outs[3:]


def _mixer_sample_kernel(x_ref, mod_ref, g_ref, win_ref, wdw_ref, bdw_ref, lng_ref, lnb_ref,
                         wpool_ref, pscale_ref, wout_ref, sconv_ref, spool_ref,
                         o_ref, oconv_ref, opool_ref, mix_scr):
    tt, bb, d = x_ref.shape
    m = tt * bb
    n_conv, d_conv = sconv_ref.shape[1], sconv_ref.shape[3]
    n_pool = spool_ref.shape[1]

    x = x_ref[...]
    shift, scale, gate = mod_ref[0], mod_ref[1], mod_ref[2]
    hm = (x * _rms_scale(x)) * g_ref[...] * (1.0 + scale) + shift
    proj = _dot(hm.reshape(m, d).astype(BF16), win_ref[...])
    u_conv = proj[:, :d_conv] * _sigmoid(proj[:, d_conv:2 * d_conv])
    u_pool = proj[:, 2 * d_conv:]

    def conv_in(s):
        return sconv_ref[0, s] if s < n_conv else u_conv[(s - n_conv) * bb:(s - n_conv + 1) * bb, :]

    def pool_in(s, lanes=slice(None)):
        return spool_ref[0, s, :, lanes] if s < n_pool else u_pool[(s - n_pool) * bb:(s - n_pool + 1) * bb, lanes]

    for s in range(n_conv):
        oconv_ref[0, s] = conv_in(s + tt)
    for s in range(n_pool):
        opool_ref[0, s] = pool_in(s + tt)

    for t in range(tt):
        acc = jnp.broadcast_to(bdw_ref[...], (bb, d_conv))
        for k in range(CONV_WIDTH):
            acc = acc + wdw_ref[k:k + 1, :] * conv_in(t + k)
        mix_scr[t * bb:(t + 1) * bb, 0:d_conv] = _layernorm_silu(acc, lng_ref[...], lnb_ref[...]).astype(BF16)

    for gi, w in enumerate(POOL_WINDOWS):
        lanes = slice(gi * LANES, (gi + 1) * LANES)
        count = float(min(SAMPLE_POS0 + 1, w))
        for t in range(tt):
            cur = pool_in(n_pool + t, lanes)
            win_sum = cur
            for j in range(1, w):
                win_sum = win_sum + pool_in(n_pool + t - j, lanes)
            pooled = win_sum / count - cur
            mixed = _dot(pooled.astype(BF16), wpool_ref[gi])
            mix_scr[t * bb:(t + 1) * bb, d_conv + gi * LANES:d_conv + (gi + 1) * LANES] = (
                mixed * pscale_ref[:, lanes]).astype(BF16)

    o_ref[...] = x + gate * _dot(mix_scr[...], wout_ref[...]).reshape(tt, bb, d)


def _mixer_sample(x, mod4, gain, w_in, w_dw, b_dw, ln_g, ln_b, w_pool, pool_scale, w_out,
                  state_conv, state_pool, *, batch_block):
    t, b, d = x.shape
    d_conv = w_dw.shape[1]
    d_pool = pool_scale.shape[1]
    n_groups, group = w_pool.shape[0], w_pool.shape[1]
    assert group == LANES and d_pool == n_groups * LANES and d_conv == d_pool
    x_spec = pl.BlockSpec((t, batch_block, d), lambda i: (0, i, 0))
    conv_spec = pl.BlockSpec((1, CONV_WIDTH - 1, batch_block, d_conv), lambda i: (0, 0, i, 0))
    pool_spec = pl.BlockSpec((1, POOL_MAX - 1, batch_block, d_pool), lambda i: (0, 0, i, 0))
    return pl.pallas_call(
        _mixer_sample_kernel,
        grid=(b // batch_block,),
        in_specs=[
            x_spec,
            pl.BlockSpec((3, 1, batch_block, d), lambda i: (1, 0, i, 0)),
            _const_spec((1, d)),
            _const_spec((d, 2 * d_conv + d_pool)),
            _const_spec((CONV_WIDTH, d_conv)),
            _const_spec((1, d_conv)),
            _const_spec((1, d_conv)),
            _const_spec((1, d_conv)),
            _const_spec((n_groups, group, group)),
            _const_spec((1, d_pool)),
            _const_spec((d_conv + d_pool, d)),
            conv_spec,
            pool_spec,
        ],
        out_specs=[x_spec, conv_spec, pool_spec],
        out_shape=[
            jax.ShapeDtypeStruct(x.shape, F32),
            jax.ShapeDtypeStruct(state_conv.shape, F32),
            jax.ShapeDtypeStruct(state_pool.shape, F32),
        ],
        scratch_shapes=[pltpu.VMEM((t * batch_block, d_conv + d_pool), BF16)],
        compiler_params=pltpu.CompilerParams(
            dimension_semantics=("arbitrary",), vmem_limit_bytes=VMEM_LIMIT_BYTES),
        name="mixer_sample",
    )(x, mod4, gain, w_in, w_dw, b_dw, ln_g, ln_b, w_pool, pool_scale, w_out, state_conv, state_pool)


def kernel(x_prompt, x_sample, state_conv, state_pool, c_prompt, c_sample, w_ada, b_ada, g_ffn1, w_ffn1_up, w_ffn1_down, g_mix, w_in, w_dw, b_dw, ln_g, ln_b, w_pool, pool_scale, w_out, g_ffn2, w_ffn2_up, w_ffn2_down, g_final):
    depth = w_ada.shape[0]
    assert depth == 1 and state_conv.shape[0] == 1 and state_pool.shape[0] == 1
    bp, tp, d = x_prompt.shape
    bs, ts, _ = x_sample.shape

    mod_p, mod_s, (w1u, w1d) = _adaln(c_prompt, c_sample, w_ada[0], b_ada[0], col_block=d, cast_slabs=8,
                                      casts=(w_ffn1_up[0], w_ffn1_down[0]))
    mod_p = mod_p.reshape(N_MOD, bp, 1, d)
    mod_s = mod_s.reshape(N_MOD, 1, bs, d)

    g_final2 = g_final.reshape(1, d)
    n_groups, group, _ = w_pool.shape[1:]

    tiles = dict(tile_p=min(tp, 1024), seq_s=min(bs, 64))
    xs = jnp.transpose(x_sample, (1, 0, 2))
    sconv = jnp.transpose(state_conv, (0, 2, 1, 3))
    spool = jnp.transpose(state_pool, (0, 2, 1, 3))

    later = (w_in[0], w_out[0], w_pool[0].reshape(n_groups * group, group))
    h_p, h_s, (w_in_b, w_out_b, w_pool_b) = _ffn(
        x_prompt, xs, mod_p, mod_s, 0, g_ffn1, w1u, w1d, None, casts=later, **tiles)
    mixer_w = (g_mix, w_in_b, w_dw[0], b_dw, ln_g, ln_b, w_pool_b.reshape(n_groups, group, group),
               pool_scale, w_out_b)

    h_p, conv_p, pool_p, (w2u, w2d) = _mixer_prompt(
        h_p, mod_p, *mixer_w, time_block=min(tp, 1024), sub_block=min(tp, 256),
        casts=(w_ffn2_up[0], w_ffn2_down[0]))
    h_s, conv_s, pool_s = _mixer_sample(h_s, mod_s, *mixer_w, sconv, spool, batch_block=min(bs, 32))

    y_p, y_s, _ = _ffn(h_p, h_s, mod_p, mod_s, 2, g_ffn2, w2u, w2d, g_final2, out_batch_major=True,
                       prompt_in_place=True, **tiles)
    return (y_p, y_s, conv_p, pool_p,
            jnp.transpose(conv_s, (0, 2, 1, 3)), jnp.transpose(pool_s, (0, 2, 1, 3)))
```
